```python
import math
import jax, jax.numpy as jnp
from jax import lax
import numpy as np

D_MODEL = 1024
BATCH = 8
SEQ = 4096
DEPTH = 2

N_MIXERS = 2
EPS = 1e-6

HYENA_ORDER = 2
HYENA_EMB_DIM = 33
HYENA_BANDS = (HYENA_EMB_DIM - 1) // 2
HYENA_FILTER_HIDDEN = 64
HYENA_SHORT_CONV = 3
HYENA_FAST_DECAY_PCT = 0.3
HYENA_SLOW_DECAY_PCT = 1.5
HYENA_DECAY_TARGET = 1e-2
HYENA_N_DIRS = 2

MLA_HEADS = 8
QK_NOPE = 128
QK_ROPE = 64
V_HEAD = 128
Q_LORA = 256
KV_LORA = 128
ROPE_THETA = 10000.0
Q_BLOCK = 128
SOFTMAX_SCALE = 1.0 / math.sqrt(QK_NOPE + QK_ROPE)

N_EXPERTS = 16
EXPERT_FF = 1024
EC_CAPACITY_FACTOR = 2

kernel_name = 'hybrid_hyena_mla_ecmoe_encoder'


def rmsnorm(x, g):
    xf = x.astype(jnp.float32)
    y = xf * lax.rsqrt(jnp.mean(xf * xf, axis=-1, keepdims=True) + EPS)
    return (y * g.astype(jnp.float32)).astype(x.dtype)


def modulate(h, shift, scale):
    return h * (1 + scale[:, None, :]) + shift[:, None, :]


def short_conv(u, w, b):
    up = jnp.pad(u, ((0, 0), (1, 1), (0, 0)))
    return w[0] * up[:, :-2] + w[1] * up[:, 1:-1] + w[2] * up[:, 2:] + b


def hyena_pos_features(L):
    t = jnp.linspace(0.0, 1.0, L, dtype=jnp.float32)[:, None]
    w = 2.0 * math.pi * jnp.arange(L, dtype=jnp.float32)[:, None] / L
    f = jnp.linspace(1e-4, HYENA_BANDS - 1, HYENA_BANDS, dtype=jnp.float32)[None, :]
    ang = f * w
    return jnp.concatenate([t, jnp.cos(ang), -jnp.sin(ang)], axis=-1)


def hyena_filters(L, w1, b1, w2, b2, w3, freq):
    z = hyena_pos_features(L)
    fr = freq.astype(jnp.float32)
    h = jnp.sin(fr * (z @ w1.astype(jnp.float32) + b1.astype(jnp.float32)))
    h = jnp.sin(fr * (h @ w2.astype(jnp.float32) + b2.astype(jnp.float32)))
    h = (h @ w3.astype(jnp.float32)).reshape(L, HYENA_ORDER - 1, HYENA_N_DIRS, D_MODEL)
    t = jnp.linspace(0.0, 1.0, L, dtype=jnp.float32)[:, None]
    min_decay = math.log(HYENA_DECAY_TARGET) / HYENA_FAST_DECAY_PCT
    max_decay = math.log(HYENA_DECAY_TARGET) / HYENA_SLOW_DECAY_PCT
    deltas = jnp.linspace(min_decay, max_decay, D_MODEL, dtype=jnp.float32)[None, :]
    decay = jnp.exp(-t * jnp.abs(deltas))
    return h * decay[:, None, None, :]


def bidir_long_conv(z, h_fwd, h_bwd, bias):
    L = z.shape[1]
    k = jnp.concatenate([h_fwd, jnp.zeros_like(h_fwd[:1]), h_bwd[:0:-1]], axis=0)
    kf = jnp.fft.rfft(k, n=2 * L, axis=0)
    zf = jnp.fft.rfft(z.astype(jnp.float32), n=2 * L, axis=1)
    y = jnp.fft.irfft(zf * kf[None], n=2 * L, axis=1)[:, :L]
    return (y + z.astype(jnp.float32) * bias.astype(jnp.float32)).astype(z.dtype)


def hyena_mixer(h, w_in, b_in, conv_w, conv_b, f_w1, f_b1, f_w2, f_b2, f_w3, f_freq, f_bias, w_out, b_out):
    L = h.shape[1]
    u = short_conv(h @ w_in + b_in, conv_w, conv_b)
    parts = jnp.split(u, HYENA_ORDER + 1, axis=-1)
    filt = hyena_filters(L, f_w1, f_b1, f_w2, f_b2, f_w3, f_freq)
    z = parts[-1]
    for o in range(HYENA_ORDER - 1):
        z = z * parts[HYENA_ORDER - 1 - o]
        z = bidir_long_conv(z, filt[:, o, 0], filt[:, o, 1], f_bias[o])
    y = z * parts[0]
    return y @ w_out + b_out


def apply_rope(x, cos, sin):
    half = x.shape[-1] // 2
    x1, x2 = x[..., :half], x[..., half:]
    return jnp.concatenate([x1 * cos - x2 * sin, x1 * sin + x2 * cos], axis=-1).astype(x.dtype)


def mla_mixer(h, positions, w_in, q_norm_g, w_qb, kv_norm_g, w_kvb, w_out):
    B, S, _ = h.shape
    a = h @ w_in
    cq, ckv, k_pe = jnp.split(a, [Q_LORA, Q_LORA + KV_LORA], axis=-1)
    q = (rmsnorm(cq, q_norm_g) @ w_qb).reshape(B, S, MLA_HEADS, QK_NOPE + QK_ROPE)
    q_nope, q_pe = q[..., :QK_NOPE], q[..., QK_NOPE:]
    kv = (rmsnorm(ckv, kv_norm_g) @ w_kvb).reshape(B, S, MLA_HEADS, QK_NOPE + V_HEAD)
    k_nope, v = kv[..., :QK_NOPE], kv[..., QK_NOPE:]
    inv_freq = ROPE_THETA ** (-jnp.arange(0, QK_ROPE, 2, dtype=jnp.float32) / QK_ROPE)
    ang = positions.astype(jnp.float32)[..., None] * inv_freq
    cos, sin = jnp.cos(ang), jnp.sin(ang)
    q_pe = apply_rope(q_pe, cos[:, :, None, :], sin[:, :, None, :])
    k_pe = apply_rope(k_pe, cos, sin)
    nb = S // Q_BLOCK
    qn_b = q_nope.reshape(B, nb, Q_BLOCK, MLA_HEADS, QK_NOPE).swapaxes(0, 1)
    qp_b = q_pe.reshape(B, nb, Q_BLOCK, MLA_HEADS, QK_ROPE).swapaxes(0, 1)

    def attend(blk):
        qn, qp = blk
        s = jnp.einsum('bqhd,bkhd->bhqk', qn, k_nope) + jnp.einsum('bqhr,bkr->bhqk', qp, k_pe)
        p = jax.nn.softmax(s.astype(jnp.float32) * SOFTMAX_SCALE, axis=-1).astype(v.dtype)
        return jnp.einsum('bhqk,bkhd->bqhd', p, v)

    o = lax.map(attend, (qn_b, qp_b))
    o = o.swapaxes(0, 1).reshape(B, S, MLA_HEADS * V_HEAD)
    return o @ w_out


def ec_moe(h, w_router, w_gate, w_up, w_down):
    B, T, D = h.shape
    C = EC_CAPACITY_FACTOR * T // N_EXPERTS
    aff = jax.nn.softmax(jnp.einsum('btd,de->bte', h, w_router).astype(jnp.float32), axis=-1)
    g, idx = lax.top_k(aff.swapaxes(1, 2), C)
    xs = jax.vmap(lambda hb, ib: hb[ib])(h, idx)
    hid = jax.nn.silu(jnp.einsum('becd,edf->becf', xs, w_gate)) * jnp.einsum('becd,edf->becf', xs, w_up)
    ye = jnp.einsum('becf,efd->becd', hid, w_down) * g[..., None].astype(h.dtype)
    return jax.vmap(lambda yb, ib: jnp.zeros((T, D), yb.dtype).at[ib.reshape(-1)].add(yb.reshape(-1, D)))(ye, idx)


def setup_inputs(seed: int = 0) -> dict:
    key = jax.random.key(seed)
    ks = jax.random.split(key, 40)
    D = D_MODEL
    nh = (DEPTH + 1) // 2
    nm = DEPTH // 2
    nrm = lambda k, shape, s: jax.random.normal(k, shape, jnp.float32) * s
    HID = HYENA_FILTER_HIDDEN
    return {
        'x': nrm(ks[0], (BATCH, SEQ, D), 1.0),
        'c': nrm(ks[1], (BATCH, D), 1.0),
        'positions': jnp.broadcast_to(jnp.arange(SEQ, dtype=jnp.int32), (BATCH, SEQ)),
        'ada_w': nrm(ks[2], (DEPTH, D, 6 * D), 0.5 * D ** -0.5),
        'ada_b': nrm(ks[3], (DEPTH, 6 * D), 0.02),
        'norm_mix_g': 1.0 + nrm(ks[4], (DEPTH, D), 0.02),
        'norm_ffn_g': 1.0 + nrm(ks[5], (DEPTH, D), 0.02),
        'hy_w_in': nrm(ks[6], (nh, D, (HYENA_ORDER + 1) * D), D ** -0.5),
        'hy_b_in': nrm(ks[7], (nh, (HYENA_ORDER + 1) * D), 0.02),
        'hy_conv_w': nrm(ks[8], (nh, HYENA_SHORT_CONV, (HYENA_ORDER + 1) * D), HYENA_SHORT_CONV ** -0.5),
        'hy_conv_b': nrm(ks[9], (nh, (HYENA_ORDER + 1) * D), 0.02),
        'hy_f_w1': nrm(ks[10], (nh, HYENA_EMB_DIM, HID), HYENA_EMB_DIM ** -0.5),
        'hy_f_b1': nrm(ks[11], (nh, HID), 0.1),
        'hy_f_w2': nrm(ks[12], (nh, HID, HID), HID ** -0.5),
        'hy_f_b2': nrm(ks[13], (nh, HID), 0.1),
        'hy_f_w3': nrm(ks[14], (nh, HID, (HYENA_ORDER - 1) * HYENA_N_DIRS * D), 0.02 * HID ** -0.5),
        'hy_f_freq': 1.0 + nrm(ks[15], (nh, HID), 0.02),
        'hy_f_bias': nrm(ks[16], (nh, HYENA_ORDER - 1, D), 0.1),
        'hy_w_out': nrm(ks[17], (nh, D, D), D ** -0.5),
        'hy_b_out': nrm(ks[18], (nh, D), 0.02),
        'mla_w_in': nrm(ks[19], (nm, D, Q_LORA + KV_LORA + QK_ROPE), D ** -0.5),
        'mla_q_norm_g': 1.0 + nrm(ks[20], (nm, Q_LORA), 0.02),
        'mla_w_qb': nrm(ks[21], (nm, Q_LORA, MLA_HEADS * (QK_NOPE + QK_ROPE)), Q_LORA ** -0.5),
        'mla_kv_norm_g': 1.0 + nrm(ks[22], (nm, KV_LORA), 0.02),
        'mla_w_kvb': nrm(ks[23], (nm, KV_LORA, MLA_HEADS * (QK_NOPE + V_HEAD)), KV_LORA ** -0.5),
        'mla_w_out': nrm(ks[24], (nm, MLA_HEADS * V_HEAD, D), (MLA_HEADS * V_HEAD) ** -0.5),
        'moe_w_router': nrm(ks[25], (DEPTH, D, N_EXPERTS), D ** -0.5),
        'moe_w_gate': nrm(ks[26], (DEPTH, N_EXPERTS, D, EXPERT_FF), D ** -0.5),
        'moe_w_up': nrm(ks[27], (DEPTH, N_EXPERTS, D, EXPERT_FF), D ** -0.5),
        'moe_w_down': nrm(ks[28], (DEPTH, N_EXPERTS, EXPERT_FF, D), EXPERT_FF ** -0.5),
        'final_norm_g': 1.0 + nrm(ks[29], (D,), 0.02),
    }


def reference(x, c, positions, ada_w, ada_b, norm_mix_g, norm_ffn_g,
              hy_w_in, hy_b_in, hy_conv_w, hy_conv_b, hy_f_w1, hy_f_b1, hy_f_w2, hy_f_b2,
              hy_f_w3, hy_f_freq, hy_f_bias, hy_w_out, hy_b_out,
              mla_w_in, mla_q_norm_g, mla_w_qb, mla_kv_norm_g, mla_w_kvb, mla_w_out,
              moe_w_router, moe_w_gate, moe_w_up, moe_w_down, final_norm_g):
    cs = jax.nn.silu(c)
    for i in range(DEPTH):
        mod = cs @ ada_w[i] + ada_b[i]
        sh1, sc1, g1, sh2, sc2, g2 = jnp.split(mod, 6, axis=-1)
        hm = modulate(rmsnorm(x, norm_mix_g[i]), sh1, sc1)
        j = i // N_MIXERS
        if i % N_MIXERS == 0:
            y = hyena_mixer(hm, hy_w_in[j], hy_b_in[j], hy_conv_w[j], hy_conv_b[j],
                            hy_f_w1[j], hy_f_b1[j], hy_f_w2[j], hy_f_b2[j], hy_f_w3[j],
                            hy_f_freq[j], hy_f_bias[j], hy_w_out[j], hy_b_out[j])
        else:
            y = mla_mixer(hm, positions, mla_w_in[j], mla_q_norm_g[j], mla_w_qb[j],
                          mla_kv_norm_g[j], mla_w_kvb[j], mla_w_out[j])
        x = x + g1[:, None, :] * y
        hf = modulate(rmsnorm(x, norm_ffn_g[i]), sh2, sc2)
        x = x + g2[:, None, :] * ec_moe(hf, moe_w_router[i], moe_w_gate[i], moe_w_up[i], moe_w_down[i])
    return rmsnorm(x, final_norm_g)
```

```python
import functools
import math

import numpy as np
import jax
import jax.numpy as jnp
from jax import lax
from jax.experimental import pallas as pl
from jax.experimental.pallas import tpu as pltpu

F32 = jnp.float32
BF16 = jnp.bfloat16
I32 = jnp.int32

EPS = 1e-6
LANES = 128
VMEM_LIMIT = 56 * 1024 * 1024

HY_BANDS = 16
HY_FAST_DECAY_PCT = 0.3
HY_SLOW_DECAY_PCT = 1.5
HY_DECAY_TARGET = 1e-2

MLA_HEADS = 8
QK_NOPE = 128
QK_ROPE = 64
V_HEAD = 128
Q_LORA = 256
KV_LORA = 128
ROPE_THETA = 10000.0
SOFTMAX_SCALE = 1.0 / math.sqrt(QK_NOPE + QK_ROPE)
HEAD_PAD = 256

EC_CAPACITY_FACTOR = 2

NT_DIMS = (((1,), (1,)), ((), ()))


def _params(*sem):
    return pltpu.CompilerParams(dimension_semantics=sem, vmem_limit_bytes=VMEM_LIMIT)


def _dot(a, b):
    return jnp.dot(a, b, preferred_element_type=F32)


def _dot_nt(a, b):
    return lax.dot_general(a, b, NT_DIMS, preferred_element_type=F32)


def _split(a):
    hi = a.astype(BF16)
    lo = (a - hi.astype(F32)).astype(BF16)
    return hi, lo


def _dot3(a, b):
    ah, al = _split(a)
    bh, bl = _split(b)
    return _dot(ah, bh) + (_dot(ah, bl) + _dot(al, bh))


def _rms(x, g):
    return x * lax.rsqrt(jnp.mean(x * x, axis=-1, keepdims=True) + EPS) * g


def _norm_mod(x, g, shift, scale):
    return _rms(x, g) * (1.0 + scale) + shift


def _ada_kernel(c_ref, w_ref, b_ref, o_ref):
    c = c_ref[...]
    cs = c / (1.0 + jnp.exp(-c))
    o_ref[0] = _dot(cs.astype(BF16), w_ref[0].astype(BF16)) + b_ref[0]


def _ada(c, ada_w, ada_b):
    depth, d, n = ada_w.shape
    b = c.shape[0]
    tn = n // 4
    return pl.pallas_call(
        _ada_kernel,
        grid=(depth, n // tn),
        in_specs=[pl.BlockSpec((b, d), lambda i, j: (0, 0)),
                  pl.BlockSpec((1, d, tn), lambda i, j: (i, 0, j)),
                  pl.BlockSpec((1, 1, tn), lambda i, j: (i, 0, j))],
        out_specs=pl.BlockSpec((1, b, tn), lambda i, j: (i, 0, j)),
        out_shape=jax.ShapeDtypeStruct((depth, b, n), F32),
        compiler_params=_params("parallel", "parallel"),
        name="ada_mod",
    )(c, ada_w, ada_b.reshape(depth, 1, n))


def _hy_inproj_kernel(x_ref, m_ref, g_ref, wt_ref, b_ref, o_ref):
    m = m_ref[0]
    h = _norm_mod(x_ref[0], g_ref[...], m[0:1], m[1:2])
    o_ref[0] = _dot_nt(wt_ref[...], h.astype(BF16)) + b_ref[...]


def _hy_inproj(x, mod, g, w_in_t, b_in, tt):
    b, s, d = x.shape
    n = w_in_t.shape[0]
    return pl.pallas_call(
        _hy_inproj_kernel,
        grid=(b, s // tt),
        in_specs=[pl.BlockSpec((1, tt, d), lambda i, j: (i, j, 0)),
                  pl.BlockSpec((1, 6, d), lambda i, j: (i, 0, 0)),
                  pl.BlockSpec((1, d), lambda i, j: (0, 0)),
                  pl.BlockSpec((n, d), lambda i, j: (0, 0)),
                  pl.BlockSpec((n, 1), lambda i, j: (0, 0))],
        out_specs=pl.BlockSpec((1, n, tt), lambda i, j: (i, 0, j)),
        out_shape=jax.ShapeDtypeStruct((b, n, s), F32),
        compiler_params=_params("parallel", "parallel"),
        name="hy_inproj",
    )(x, mod, g.reshape(1, d), w_in_t, b_in.reshape(n, 1))


def _short_conv(u, w):
    length = u.shape[1]
    lane = lax.broadcasted_iota(I32, u.shape, 1)
    prev = jnp.where(lane == 0, 0.0, pltpu.roll(u, 1, 1))
    nxt = jnp.where(lane == length - 1, 0.0, pltpu.roll(u, length - 1, 1))
    return w[:, 0:1] * prev + w[:, 1:2] * u + w[:, 2:3] * nxt + w[:, 3:4]


def _hy_conv_gate_kernel(x0_ref, x1_ref, v_ref, w0_ref, w1_ref, wv_ref, z_ref, x0c_ref):
    z_ref[0] = _short_conv(v_ref[0], wv_ref[...]) * _short_conv(x1_ref[0], w1_ref[...])
    x0c_ref[0] = _short_conv(x0_ref[0], w0_ref[...])


def _hy_conv_gate(u_t, cwb, d, ct):
    b, _, s = u_t.shape
    nb = d // ct
    u_spec = lambda p: pl.BlockSpec((1, ct, s), lambda i, j, p=p: (i, p * nb + j, 0))
    w_spec = lambda p: pl.BlockSpec((ct, 4), lambda i, j, p=p: (p * nb + j, 0))
    out_spec = pl.BlockSpec((1, ct, s), lambda i, j: (i, j, 0))
    return pl.pallas_call(
        _hy_conv_gate_kernel,
        grid=(b, nb),
        in_specs=[u_spec(0), u_spec(1), u_spec(2), w_spec(0), w_spec(1), w_spec(2)],
        out_specs=[out_spec, out_spec],
        out_shape=[jax.ShapeDtypeStruct((b, d, s), F32)] * 2,
        compiler_params=_params("parallel", "parallel"),
        name="hy_conv_gate",
    )(u_t, u_t, u_t, cwb, cwb, cwb)


def _hy_filter_kernel(w1t_ref, b1_ref, w2t_ref, b2_ref, w3t_ref, fr_ref, o_ref, *, length, lt, nfeat):
    half = pl.program_id(0)
    j = pl.program_id(1)
    d = o_ref.shape[0]
    rows = w1t_ref.shape[1]
    hi = lax.Precision.HIGHEST
    t = half * length + j * lt + lax.broadcasted_iota(I32, (rows, lt), 1)
    pos = jnp.where(half == 0, t, 2 * length - t).astype(F32)
    r = lax.broadcasted_iota(I32, (rows, lt), 0)
    tlin = pos / (length - 1.0)
    w = 2.0 * math.pi * pos / length
    fidx = jnp.where(r > HY_BANDS, r - (HY_BANDS + 1), r - 1).astype(F32)
    f = 1e-4 + fidx * ((HY_BANDS - 1 - 1e-4) / (HY_BANDS - 1))
    ang = f * w
    feat = jnp.where(r == 0, tlin, jnp.where(r <= HY_BANDS, jnp.cos(ang), -jnp.sin(ang)))
    feat = jnp.where(r < nfeat, feat, 0.0)
    fr = fr_ref[...]
    h = jnp.sin(fr * (jnp.dot(w1t_ref[...], feat, precision=hi, preferred_element_type=F32) + b1_ref[...]))
    h = jnp.sin(fr * (jnp.dot(w2t_ref[...], h, precision=hi, preferred_element_type=F32) + b2_ref[...]))
    out = jnp.dot(w3t_ref[...], h, precision=hi, preferred_element_type=F32)
    c = lax.broadcasted_iota(I32, (d, lt), 0).astype(F32)
    min_decay = math.log(HY_DECAY_TARGET) / HY_FAST_DECAY_PCT
    max_decay = math.log(HY_DECAY_TARGET) / HY_SLOW_DECAY_PCT
    delta = min_decay + c * ((max_decay - min_decay) / (d - 1.0))
    decay = jnp.exp(-tlin[0:1, :] * jnp.abs(delta))
    valid = pos[0:1, :] < length
    o_ref[...] = jnp.where(valid, out * decay, 0.0)


def _hy_filter(f_w1, f_b1, f_w2, f_b2, f_w3, f_freq, d, length):
    nfeat, hid = f_w1.shape
    rows = ((nfeat + 7) // 8) * 8
    w1t = jnp.zeros((hid, rows), F32).at[:, :nfeat].set(f_w1.T)
    lt = min(512, length)
    col = lambda a: a.reshape(hid, 1)
    small = lambda shape: pl.BlockSpec(shape, lambda h, j: (0, 0))
    return pl.pallas_call(
        functools.partial(_hy_filter_kernel, length=length, lt=lt, nfeat=nfeat),
        grid=(2, length // lt),
        in_specs=[small((hid, rows)), small((hid, 1)), small((hid, hid)), small((hid, 1)),
                  pl.BlockSpec((d, hid), lambda h, j: (h, 0)), small((hid, 1))],
        out_specs=pl.BlockSpec((d, lt), lambda h, j: (0, h * (length // lt) + j)),
        out_shape=jax.ShapeDtypeStruct((d, 2 * length), F32),
        compiler_params=_params("parallel", "parallel"),
        name="hy_filter",
    )(w1t, col(f_b1), f_w2.T, col(f_b2), f_w3.T, col(f_freq))


def _dft_tables(length):
    n = 2 * length
    n1 = n // LANES
    r = n1 // 2
    k1 = np.arange(n1)[:, None]
    ang1 = 2.0 * np.pi * k1 * np.arange(n1)[None, :] / n1
    c1, s1 = np.cos(ang1), np.sin(ang1)
    m1_real = np.concatenate([c1, -s1], axis=0)
    m1_cplx = np.block([[c1[:, :r], s1[:, :r]], [-s1[:, :r], c1[:, :r]]])
    m1_inv = np.block([[c1[:r, :], -s1[:r, :]], [s1[:r, :], c1[:r, :]]])
    ang_t = 2.0 * np.pi * k1 * np.arange(LANES)[None, :] / n
    tw = np.stack([np.cos(ang_t), np.sin(ang_t)])
    n2 = np.arange(LANES)
    ang2 = 2.0 * np.pi * n2[:, None] * n2[None, :] / LANES
    c2, s2 = np.cos(ang2), np.sin(ang2)
    w2_fwd = np.block([[c2, -s2], [s2, c2]])
    w2_inv = np.block([[c2, s2], [-s2, c2]])
    f = lambda a: jnp.asarray(a, F32)
    return dict(n1=n1, r=r, m1_real=f(m1_real), m1_cplx=f(m1_cplx), m1_inv=f(m1_inv),
                tw=f(tw), w2_fwd=f(w2_fwd), w2_inv=f(w2_inv))


def _wide(ref3, ct):
    return jnp.concatenate([ref3[c] for c in range(ct)], axis=1)


def _fft_fwd(x_wide, m1, tc, ts, w2, ct, n1):
    a = _dot3(m1, x_wide)
    ar, ai = a[:n1], a[n1:]
    tall = []
    for c in range(ct):
        arc, aic = ar[:, c * LANES:(c + 1) * LANES], ai[:, c * LANES:(c + 1) * LANES]
        tall.append(jnp.concatenate([arc * tc + aic * ts, aic * tc - arc * ts], axis=1))
    return _dot3(jnp.concatenate(tall, axis=0), w2)


def _fft_inv(y_tall, w2i, tc, ts, m1i, ct, n1):
    cc = _dot3(y_tall, w2i)
    dr, di = [], []
    for c in range(ct):
        cr, ci = cc[c * n1:(c + 1) * n1, :LANES], cc[c * n1:(c + 1) * n1, LANES:]
        dr.append(cr * tc - ci * ts)
        di.append(ci * tc + cr * ts)
    d_wide = jnp.concatenate([jnp.concatenate(dr, axis=1), jnp.concatenate(di, axis=1)], axis=0)
    return _dot3(m1i, d_wide)


def _hy_filter_fft_kernel(k_ref, m1_ref, tw_ref, w2_ref, kr_ref, ki_ref, *, ct, n1):
    x = _fft_fwd(_wide(k_ref, ct), m1_ref[...], tw_ref[0], tw_ref[1], w2_ref[...], ct, n1)
    x = x * (1.0 / (n1 * LANES))
    kr_ref[...] = x[:, :LANES].reshape(ct, n1, LANES)
    ki_ref[...] = x[:, LANES:].reshape(ct, n1, LANES)


def _hy_filter_fft(k4, tabs, ct):
    d, n1, _ = k4.shape
    const = lambda a: pl.BlockSpec(a.shape, lambda i, nd=a.ndim: (0,) * nd)
    blk = pl.BlockSpec((ct, n1, LANES), lambda i: (i, 0, 0))
    return pl.pallas_call(
        functools.partial(_hy_filter_fft_kernel, ct=ct, n1=n1),
        grid=(d // ct,),
        in_specs=[blk, const(tabs["m1_real"]), const(tabs["tw"]), const(tabs["w2_fwd"])],
        out_specs=[blk, blk],
        out_shape=[jax.ShapeDtypeStruct((d, n1, LANES), F32)] * 2,
        compiler_params=_params("parallel"),
        name="hy_filter_fft",
    )(k4, tabs["m1_real"], tabs["tw"], tabs["w2_fwd"])


def _hy_longconv_kernel(z_ref, x0_ref, kr_ref, ki_ref, bias_ref, m1_ref, m1i_ref, tw_ref, w2_ref, w2i_ref,
                        o_ref, *, ct, n1, r):
    tc, ts = tw_ref[0], tw_ref[1]
    x_wide = jnp.concatenate([_wide(z_ref.at[0], ct), _wide(z_ref.at[1], ct)], axis=0)
    x = _fft_fwd(x_wide, m1_ref[...], tc, ts, w2_ref[...], ct, n1)
    xr, xi = x[:, :LANES], x[:, LANES:]
    kr = kr_ref[...].reshape(ct * n1, LANES)
    ki = ki_ref[...].reshape(ct * n1, LANES)
    y_tall = jnp.concatenate([xr * kr - xi * ki, xr * ki + xi * kr], axis=1)
    y = _fft_inv(y_tall, w2i_ref[...], tc, ts, m1i_ref[...], ct, n1)
    for p in range(2):
        for c in range(ct):
            conv = y[p * r:(p + 1) * r, c * LANES:(c + 1) * LANES]
            o_ref[p, c] = (conv + bias_ref[c] * z_ref[p, c]) * x0_ref[p, c]


def _hy_longconv(z4, x0c4, kr, ki, bias, tabs, ct):
    b, d, r, _ = z4.shape
    n1 = tabs["n1"]
    const = lambda a: pl.BlockSpec(a.shape, lambda i, j, nd=a.ndim: (0,) * nd)
    data = pl.BlockSpec((2, ct, r, LANES), lambda i, j: (j, i, 0, 0))
    spec = pl.BlockSpec((ct, n1, LANES), lambda i, j: (i, 0, 0))
    consts = [tabs["m1_cplx"], tabs["m1_inv"], tabs["tw"], tabs["w2_fwd"], tabs["w2_inv"]]
    return pl.pallas_call(
        functools.partial(_hy_longconv_kernel, ct=ct, n1=n1, r=r),
        grid=(d // ct, b // 2),
        in_specs=[data, data, spec, spec, pl.BlockSpec((ct, 1, 1), lambda i, j: (i, 0, 0))]
                 + [const(a) for a in consts],
        out_specs=data,
        out_shape=jax.ShapeDtypeStruct((b, d, r, LANES), F32),
        compiler_params=_params("parallel", "parallel"),
        name="hy_longconv",
    )(z4, x0c4, kr, ki, bias.reshape(d, 1, 1), *consts)


def _router_tail(x1, m, gf_ref, wrt_ref, x1_ref, hf_ref, aff_ref):
    x1_ref[0] = x1
    hf = _norm_mod(x1, gf_ref[...], m[3:4], m[4:5]).astype(BF16)
    hf_ref[0] = hf
    logits = _dot_nt(wrt_ref[...], hf)
    ex = jnp.exp(logits - jnp.max(logits, axis=0, keepdims=True))
    aff_ref[0] = ex / jnp.sum(ex, axis=0, keepdims=True)


def _hy_outproj_kernel(y_ref, x_ref, m_ref, w_ref, b_ref, gf_ref, wrt_ref, x1_ref, hf_ref, aff_ref):
    m = m_ref[0]
    y = jnp.transpose(y_ref[0]).astype(BF16)
    x1 = x_ref[0] + m[2:3] * (_dot(y, w_ref[...]) + b_ref[...])
    _router_tail(x1, m, gf_ref, wrt_ref, x1_ref, hf_ref, aff_ref)


def _mla_outproj_kernel(y_ref, x_ref, m_ref, w_ref, gf_ref, wrt_ref, x1_ref, hf_ref, aff_ref):
    m = m_ref[0]
    x1 = x_ref[0] + m[2:3] * _dot(y_ref[0].astype(BF16), w_ref[...])
    _router_tail(x1, m, gf_ref, wrt_ref, x1_ref, hf_ref, aff_ref)


def _outproj(y, x, mod, w_out, b_out, g_ffn, w_router_t, tt, channel_major):
    b, s, d = x.shape
    e = w_router_t.shape[0]
    dy = w_out.shape[0]
    tile = pl.BlockSpec((1, tt, d), lambda i, j: (i, j, 0))
    const = lambda shape: pl.BlockSpec(shape, lambda i, j: (0, 0))
    if channel_major:
        y_spec = pl.BlockSpec((1, dy, tt), lambda i, j: (i, 0, j))
        kern, extra, extra_specs = _hy_outproj_kernel, [b_out.reshape(1, d)], [const((1, d))]
    else:
        y_spec = pl.BlockSpec((1, tt, dy), lambda i, j: (i, j, 0))
        kern, extra, extra_specs = _mla_outproj_kernel, [], []
    return pl.pallas_call(
        kern,
        grid=(b, s // tt),
        in_specs=[y_spec, tile, pl.BlockSpec((1, 6, d), lambda i, j: (i, 0, 0)), const((dy, d))]
                 + extra_specs + [const((1, d)), const((e, d))],
        out_specs=[tile, tile, pl.BlockSpec((1, e, tt), lambda i, j: (i, 0, j))],
        out_shape=[jax.ShapeDtypeStruct((b, s, d), F32), jax.ShapeDtypeStruct((b, s, d), BF16),
                   jax.ShapeDtypeStruct((b, e, s), F32)],
        compiler_params=_params("parallel", "parallel"),
        name="outproj_router",
    )(y, x, mod, w_out, *extra, g_ffn.reshape(1, d), w_router_t)


def _route_kernel(a_ref, o_ref, *, n_exp, n_chunk, cap):
    rows = n_exp * n_chunk
    a = a_ref[0].reshape(rows, LANES)
    ri = lax.broadcasted_iota(I32, (rows, rows), 0)
    rj = lax.broadcasted_iota(I32, (rows, rows), 1)
    same = (ri // n_chunk) == (rj // n_chunk)
    blk_all = jnp.where(same, 1.0, 0.0).astype(BF16)
    blk_before = jnp.where(same & (rj < ri), 1.0, 0.0).astype(BF16)
    li = lax.broadcasted_iota(I32, (LANES, LANES), 0)
    lj = lax.broadcasted_iota(I32, (LANES, LANES), 1)
    ones = jnp.ones((LANES, LANES), BF16)
    before = jnp.where(li < lj, 1.0, 0.0).astype(BF16)

    def as_bf16(mask):
        return jnp.where(mask, 1.0, 0.0).astype(BF16)

    def expert_count(mask):
        return _dot(blk_all, _dot(as_bf16(mask), ones).astype(BF16))

    def prefix(mask):
        mb = as_bf16(mask)
        return _dot(mb, before) + _dot(blk_before, _dot(mb, ones).astype(BF16))

    cur = jnp.zeros((rows, LANES), I32)
    for bit in range(30, -1, -1):
        cand = cur | (1 << bit)
        cur = jnp.where(expert_count(a >= pltpu.bitcast(cand, F32)) >= cap, cand, cur)
    lo = pltpu.bitcast(cur, F32)
    hi = pltpu.bitcast(cur + 1, F32)
    for _ in range(16):
        mid = lo + (hi - lo) * 0.5
        take = expert_count(a >= mid) >= cap
        lo = jnp.where(take, mid, lo)
        hi = jnp.where(take, hi, mid)
    gt = a >= hi
    eq = (a >= lo) & (a < hi)
    need = cap - expert_count(gt)
    sel = gt | (eq & (prefix(eq) < need))
    slot = prefix(sel).astype(I32)
    o_ref[0] = jnp.where(sel, slot, -1).reshape(n_exp, n_chunk, LANES)


def _route(aff_t, cap):
    b, e, t = aff_t.shape
    nc = t // LANES
    blk = pl.BlockSpec((1, e, nc, LANES), lambda i: (i, 0, 0, 0))
    slot = pl.pallas_call(
        functools.partial(_route_kernel, n_exp=e, n_chunk=nc, cap=cap),
        grid=(b,),
        in_specs=[blk],
        out_specs=blk,
        out_shape=jax.ShapeDtypeStruct((b, e, nc, LANES), I32),
        compiler_params=_params("parallel"),
        name="route_topc",
    )(aff_t.reshape(b, e, nc, LANES))
    return slot.reshape(b, e, t)


def _expert_kernel(hf_ref, slot_ref, aff_ref, wg_ref, wu_ref, wd_ref, o_ref, *, cap, tk):
    t = hf_ref.shape[1]
    slot_id = lax.broadcasted_iota(I32, (cap, tk), 0)
    xs = jnp.zeros((cap, hf_ref.shape[2]), F32)
    gsl = jnp.zeros((cap, 1), F32)
    for k in range(t // tk):
        hit = slot_id == slot_ref[0, 0, :, k * tk:(k + 1) * tk]
        xs = xs + _dot(jnp.where(hit, 1.0, 0.0).astype(BF16), hf_ref[0, k * tk:(k + 1) * tk, :])
        gsl = gsl + jnp.sum(jnp.where(hit, aff_ref[0, 0, :, k * tk:(k + 1) * tk], 0.0), axis=1, keepdims=True)
    xs = xs.astype(BF16)
    gate = _dot(xs, wg_ref[0])
    up = _dot(xs, wu_ref[0])
    hid = gate / (1.0 + jnp.exp(-gate)) * up
    o_ref[0, 0] = _dot(hid.astype(BF16), wd_ref[0]) * gsl


def _experts(hf, slot_row, aff_row, wg, wu, wd, cap):
    b, t, d = hf.shape
    e, _, f = wg.shape
    tk = min(1024, t)
    row = pl.BlockSpec((1, 1, 1, t), lambda i, j: (i, j, 0, 0))
    return pl.pallas_call(
        functools.partial(_expert_kernel, cap=cap, tk=tk),
        grid=(b, e),
        in_specs=[pl.BlockSpec((1, t, d), lambda i, j: (i, 0, 0)), row, row,
                  pl.BlockSpec((1, d, f), lambda i, j: (j, 0, 0)),
                  pl.BlockSpec((1, d, f), lambda i, j: (j, 0, 0)),
                  pl.BlockSpec((1, f, d), lambda i, j: (j, 0, 0))],
        out_specs=pl.BlockSpec((1, 1, cap, d), lambda i, j: (i, j, 0, 0)),
        out_shape=jax.ShapeDtypeStruct((b, e, cap, d), F32),
        compiler_params=_params("parallel", "arbitrary"),
        name="expert_ffn",
    )(hf, slot_row, aff_row, wg, wu, wd)


def _combine_kernel(ye_ref, slot_ref, x_ref, m_ref, gn_ref, o_ref, *, cap, final_norm):
    e = pl.program_id(2)

    @pl.when(e == 0)
    def _():
        o_ref[...] = jnp.zeros_like(o_ref)

    tt = o_ref.shape[1]
    hit = slot_ref[0, 0] == lax.broadcasted_iota(I32, (tt, cap), 1)
    pt = jnp.where(hit, 1.0, 0.0).astype(BF16)
    hi, lo = _split(ye_ref[0, 0])
    o_ref[0] += _dot(pt, hi) + _dot(pt, lo)

    @pl.when(e == pl.num_programs(2) - 1)
    def _():
        x2 = x_ref[0] + m_ref[0][5:6] * o_ref[0]
        o_ref[0] = _rms(x2, gn_ref[...]) if final_norm else x2


def _combine(ye, slot_col, x1, mod, g_final, tt, final_norm):
    b, e, cap, d = ye.shape
    t = x1.shape[1]
    tile = pl.BlockSpec((1, tt, d), lambda i, j, k: (i, j, 0))
    return pl.pallas_call(
        functools.partial(_combine_kernel, cap=cap, final_norm=final_norm),
        grid=(b, t // tt, e),
        in_specs=[pl.BlockSpec((1, 1, cap, d), lambda i, j, k: (i, k, 0, 0)),
                  pl.BlockSpec((1, 1, tt, 1), lambda i, j, k: (i, k, j, 0)),
                  tile, pl.BlockSpec((1, 6, d), lambda i, j, k: (i, 0, 0)),
                  pl.BlockSpec((1, d), lambda i, j, k: (0, 0))],
        out_specs=tile,
        out_shape=jax.ShapeDtypeStruct((b, t, d), F32),
        compiler_params=_params("parallel", "parallel", "arbitrary"),
        name="moe_combine",
    )(ye, slot_col, x1, mod, g_final.reshape(1, d))


def _moe(hf, aff_t, x1, mod, wg, wu, wd, g_final, final_norm):
    b, t, d = hf.shape
    e = aff_t.shape[1]
    cap = EC_CAPACITY_FACTOR * t // e
    slot = _route(aff_t, cap)
    ye = _experts(hf, slot.reshape(b, e, 1, t), aff_t.reshape(b, e, 1, t), wg, wu, wd, cap)
    return _combine(ye, slot.reshape(b, e, t, 1), x1, mod, g_final, min(1024, t), final_norm)


def _rope(x, cos_t, sin_lo, sin_hi):
    return x * cos_t + pltpu.roll(x, LANES - QK_ROPE // 2, 1) * sin_lo + pltpu.roll(x, QK_ROPE // 2, 1) * sin_hi


def _mla_proj_kernel(x_ref, m_ref, g_ref, pos_ref, invf_ref, win_ref, qg_ref, wq_ref, kg_ref, wkn_ref, wv_ref,
                     q_ref, k_ref, v_ref):
    m = m_ref[0]
    h = _norm_mod(x_ref[0], g_ref[...], m[0:1], m[1:2]).astype(BF16)
    a = _dot(h, win_ref[...])
    cq = a[:, :Q_LORA]
    ckv = a[:, Q_LORA:Q_LORA + KV_LORA]
    kpe = a[:, Q_LORA + KV_LORA:]
    q = _dot(_rms(cq, qg_ref[...]).astype(BF16), wq_ref[...])
    kvn = _rms(ckv, kg_ref[...]).astype(BF16)
    kn = _dot(kvn, wkn_ref[...])
    v_ref[0] = _dot(kvn, wv_ref[...]).astype(BF16)
    ang = pos_ref[0] * invf_ref[...]
    lane = lax.broadcasted_iota(I32, ang.shape, 1)
    cos_t = jnp.where(lane < QK_ROPE, jnp.cos(ang), 0.0)
    sin_a = jnp.sin(ang)
    sin_lo = jnp.where(lane < QK_ROPE // 2, -sin_a, 0.0)
    sin_hi = jnp.where((lane >= QK_ROPE // 2) & (lane < QK_ROPE), sin_a, 0.0)
    kpe = _rope(kpe, cos_t, sin_lo, sin_hi)
    qs, ks = [], []
    for hd in range(MLA_HEADS):
        qs.append(q[:, hd * HEAD_PAD:hd * HEAD_PAD + QK_NOPE])
        qs.append(_rope(q[:, hd * HEAD_PAD + QK_NOPE:(hd + 1) * HEAD_PAD], cos_t, sin_lo, sin_hi))
        ks.append(kn[:, hd * QK_NOPE:(hd + 1) * QK_NOPE])
        ks.append(kpe)
    q_ref[0] = jnp.concatenate(qs, axis=1).astype(BF16)
    k_ref[0] = jnp.concatenate(ks, axis=1).astype(BF16)


def _mla_proj(x, mod, g, positions, w_in, q_g, w_qb, kv_g, w_kvb, tt):
    b, s, d = x.shape
    nh = MLA_HEADS
    pad = HEAD_PAD - QK_NOPE - QK_ROPE
    win = jnp.concatenate([w_in, jnp.zeros((d, LANES - QK_ROPE), F32)], axis=1).astype(BF16)
    wq = w_qb.reshape(Q_LORA, nh, QK_NOPE + QK_ROPE)
    wq = jnp.concatenate([wq, jnp.zeros((Q_LORA, nh, pad), F32)], axis=2).reshape(Q_LORA, nh * HEAD_PAD).astype(BF16)
    wkv = w_kvb.reshape(KV_LORA, nh, QK_NOPE + V_HEAD)
    wkn = wkv[:, :, :QK_NOPE].reshape(KV_LORA, nh * QK_NOPE).astype(BF16)
    wv = wkv[:, :, QK_NOPE:].reshape(KV_LORA, nh * V_HEAD).astype(BF16)
    inv_freq = ROPE_THETA ** (-jnp.arange(0, QK_ROPE, 2, dtype=F32) / QK_ROPE)
    invf = jnp.concatenate([inv_freq, inv_freq, jnp.zeros((LANES - QK_ROPE,), F32)]).reshape(1, LANES)
    pos = positions.astype(F32).reshape(b, s, 1)
    const = lambda a: pl.BlockSpec(a.shape, lambda i, j: (0, 0))
    tile = lambda w: pl.BlockSpec((1, tt, w), lambda i, j: (i, j, 0))
    g2, qg2, kg2 = g.reshape(1, d), q_g.reshape(1, Q_LORA), kv_g.reshape(1, KV_LORA)
    return pl.pallas_call(
        _mla_proj_kernel,
        grid=(b, s // tt),
        in_specs=[tile(d), pl.BlockSpec((1, 6, d), lambda i, j: (i, 0, 0)), const(g2), tile(1), const(invf),
                  const(win), const(qg2), const(wq), const(kg2), const(wkn), const(wv)],
        out_specs=[tile(nh * HEAD_PAD), tile(nh * HEAD_PAD), tile(nh * V_HEAD)],
        out_shape=[jax.ShapeDtypeStruct((b, s, nh * HEAD_PAD), BF16),
                   jax.ShapeDtypeStruct((b, s, nh * HEAD_PAD), BF16),
                   jax.ShapeDtypeStruct((b, s, nh * V_HEAD), BF16)],
        compiler_params=_params("parallel", "parallel"),
        name="mla_proj",
    )(x, mod, g2, pos, invf, win, qg2, wq, kg2, wkn, wv)


def _mla_attn_kernel(q_ref, k_ref, v_ref, o_ref):
    s = _dot_nt(q_ref[0], k_ref[0])
    p = jnp.exp((s - jnp.max(s, axis=1, keepdims=True)) * SOFTMAX_SCALE)
    o = _dot(p.astype(BF16), v_ref[0])
    o_ref[0] = o / jnp.sum(p, axis=1, keepdims=True)


def _mla_attn(q, k, v, tq):
    b, s, _ = q.shape
    nh = MLA_HEADS
    return pl.pallas_call(
        _mla_attn_kernel,
        grid=(b, nh, s // tq),
        in_specs=[pl.BlockSpec((1, tq, HEAD_PAD), lambda i, h, j: (i, j, h)),
                  pl.BlockSpec((1, s, HEAD_PAD), lambda i, h, j: (i, 0, h)),
                  pl.BlockSpec((1, s, V_HEAD), lambda i, h, j: (i, 0, h))],
        out_specs=pl.BlockSpec((1, tq, V_HEAD), lambda i, h, j: (i, j, h)),
        out_shape=jax.ShapeDtypeStruct((b, s, nh * V_HEAD), F32),
        compiler_params=_params("parallel", "parallel", "parallel"),
        name="mla_attn",
    )(q, k, v)


def kernel(x, c, positions, ada_w, ada_b, norm_mix_g, norm_ffn_g, hy_w_in, hy_b_in, hy_conv_w, hy_conv_b, hy_f_w1, hy_f_b1, hy_f_w2, hy_f_b2, hy_f_w3, hy_f_freq, hy_f_bias, hy_w_out, hy_b_out, mla_w_in, mla_q_norm_g, mla_w_qb, mla_kv_norm_g, mla_w_kvb, mla_w_out, moe_w_router, moe_w_gate, moe_w_up, moe_w_down, final_norm_g):
    b, s, d = x.shape
    depth = ada_w.shape[0]
    assert s % LANES == 0 and b % 2 == 0 and depth == 2
    tt = min(512, s)
    mod = _ada(c, ada_w, ada_b).reshape(depth, b, 6, d)

    u_t = _hy_inproj(x, mod[0], norm_mix_g[0], hy_w_in[0].T.astype(BF16), hy_b_in[0], tt)
    cwb = jnp.concatenate([hy_conv_w[0].T, hy_conv_b[0][:, None]], axis=1)
    z_t, x0c_t = _hy_conv_gate(u_t, cwb, d, min(64, d))
    tabs = _dft_tables(s)
    n1, r = tabs["n1"], tabs["r"]
    k_t = _hy_filter(hy_f_w1[0], hy_f_b1[0], hy_f_w2[0], hy_f_b2[0], hy_f_w3[0], hy_f_freq[0], d, s)
    ct = min(16, d)
    kr, ki = _hy_filter_fft(k_t.reshape(d, n1, LANES), tabs, ct)
    yg = _hy_longconv(z_t.reshape(b, d, r, LANES), x0c_t.reshape(b, d, r, LANES), kr, ki,
                      hy_f_bias[0, 0], tabs, ct).reshape(b, d, s)
    x1, hf, aff_t = _outproj(yg, x, mod[0], hy_w_out[0].astype(BF16), hy_b_out[0], norm_ffn_g[0],
                             moe_w_router[0].T.astype(BF16), tt, True)
    x2 = _moe(hf, aff_t, x1, mod[0], moe_w_gate[0].astype(BF16), moe_w_up[0].astype(BF16),
              moe_w_down[0].astype(BF16), final_norm_g, False)

    q, k, v = _mla_proj(x2, mod[1], norm_mix_g[1], positions, mla_w_in[0], mla_q_norm_g[0], mla_w_qb[0],
                        mla_kv_norm_g[0], mla_w_kvb[0], tt)
    o = _mla_attn(q, k, v, min(256, s))
    x3, hf, aff_t = _outproj(o, x2, mod[1], mla_w_out[0].astype(BF16), None, norm_ffn_g[1],
                             moe_w_router[1].T.astype(BF16), tt, False)
    return _moe(hf, aff_t, x3, mod[1], moe_w_gate[1].astype(BF16), moe_w_up[1].astype(BF16),
                moe_w_down[1].astype(BF16), final_norm_g, True)
```

```python
import functools
import math

import numpy as np
import jax
import jax.numpy as jnp
from jax import lax
from jax.experimental import pallas as pl
from jax.experimental.pallas import tpu as pltpu

F32 = jnp.float32
BF16 = jnp.bfloat16
I32 = jnp.int32

EPS = 1e-6
LANES = 128
VMEM_LIMIT = 56 * 1024 * 1024

HY_BANDS = 16
HY_FAST_DECAY_PCT = 0.3
HY_SLOW_DECAY_PCT = 1.5
HY_DECAY_TARGET = 1e-2

MLA_HEADS = 8
QK_NOPE = 128
QK_ROPE = 64
V_HEAD = 128
Q_LORA = 256
KV_LORA = 128
ROPE_THETA = 10000.0
SOFTMAX_SCALE = 1.0 / math.sqrt(QK_NOPE + QK_ROPE)
HEAD_PAD = 256

EC_CAPACITY_FACTOR = 2

NT_DIMS = (((1,), (1,)), ((), ()))


def _params(*sem):
    return pltpu.CompilerParams(dimension_semantics=sem, vmem_limit_bytes=VMEM_LIMIT)


def _dot(a, b):
    return jnp.dot(a, b, preferred_element_type=F32)


def _dot_nt(a, b):
    return lax.dot_general(a, b, NT_DIMS, preferred_element_type=F32)


def _split(a):
    hi = a.astype(BF16)
    lo = (a - hi.astype(F32)).astype(BF16)
    return hi, lo


def _dot3(a, b):
    ah, al = _split(a)
    bh, bl = _split(b)
    return _dot(ah, bh) + (_dot(ah, bl) + _dot(al, bh))


def _dot1(a, b):
    return _dot(a.astype(BF16), b.astype(BF16))


def _rms(x, g):
    return x * lax.rsqrt(jnp.mean(x * x, axis=-1, keepdims=True) + EPS) * g


def _norm_mod(x, g, shift, scale):
    return _rms(x, g) * (1.0 + scale) + shift


def _ada_kernel(c_ref, w_ref, b_ref, o_ref):
    c = c_ref[...]
    cs = c / (1.0 + jnp.exp(-c))
    o_ref[0] = _dot(cs.astype(BF16), w_ref[0].astype(BF16)) + b_ref[0]


def _ada(c, ada_w, ada_b):
    depth, d, n = ada_w.shape
    b = c.shape[0]
    tn = n // 4
    return pl.pallas_call(
        _ada_kernel,
        grid=(depth, n // tn),
        in_specs=[pl.BlockSpec((b, d), lambda i, j: (0, 0)),
                  pl.BlockSpec((1, d, tn), lambda i, j: (i, 0, j)),
                  pl.BlockSpec((1, 1, tn), lambda i, j: (i, 0, j))],
        out_specs=pl.BlockSpec((1, b, tn), lambda i, j: (i, 0, j)),
        out_shape=jax.ShapeDtypeStruct((depth, b, n), F32),
        compiler_params=_params("parallel", "parallel"),
        name="ada_mod",
    )(c, ada_w, ada_b.reshape(depth, 1, n))


def _hy_inproj_kernel(x_ref, m_ref, g_ref, wt_ref, b_ref, o_ref):
    m = m_ref[0]
    h = _norm_mod(x_ref[0], g_ref[...], m[0:1], m[1:2])
    o_ref[0] = _dot_nt(wt_ref[...], h.astype(BF16)) + b_ref[...]


def _hy_inproj(x, mod, g, w_in_t, b_in, tt):
    b, s, d = x.shape
    n = w_in_t.shape[0]
    return pl.pallas_call(
        _hy_inproj_kernel,
        grid=(b, s // tt),
        in_specs=[pl.BlockSpec((1, tt, d), lambda i, j: (i, j, 0)),
                  pl.BlockSpec((1, 6, d), lambda i, j: (i, 0, 0)),
                  pl.BlockSpec((1, d), lambda i, j: (0, 0)),
                  pl.BlockSpec((n, d), lambda i, j: (0, 0)),
                  pl.BlockSpec((n, 1), lambda i, j: (0, 0))],
        out_specs=pl.BlockSpec((1, n, tt), lambda i, j: (i, 0, j)),
        out_shape=jax.ShapeDtypeStruct((b, n, s), F32),
        compiler_params=_params("parallel", "parallel"),
        name="hy_inproj",
    )(x, mod, g.reshape(1, d), w_in_t, b_in.reshape(n, 1))


def _short_conv(u, w):
    length = u.shape[1]
    lane = lax.broadcasted_iota(I32, u.shape, 1)
    prev = jnp.where(lane == 0, 0.0, pltpu.roll(u, 1, 1))
    nxt = jnp.where(lane == length - 1, 0.0, pltpu.roll(u, length - 1, 1))
    return w[:, 0:1] * prev + w[:, 1:2] * u + w[:, 2:3] * nxt + w[:, 3:4]


def _hy_conv_gate_kernel(x0_ref, x1_ref, v_ref, w0_ref, w1_ref, wv_ref, z_ref, x0c_ref):
    z_ref[0] = _short_conv(v_ref[0], wv_ref[...]) * _short_conv(x1_ref[0], w1_ref[...])
    x0c_ref[0] = _short_conv(x0_ref[0], w0_ref[...])


def _hy_conv_gate(u_t, cwb, d, ct):
    b, _, s = u_t.shape
    nb = d // ct
    u_spec = lambda p: pl.BlockSpec((1, ct, s), lambda i, j, p=p: (i, p * nb + j, 0))
    w_spec = lambda p: pl.BlockSpec((ct, 4), lambda i, j, p=p: (p * nb + j, 0))
    out_spec = pl.BlockSpec((1, ct, s), lambda i, j: (i, j, 0))
    return pl.pallas_call(
        _hy_conv_gate_kernel,
        grid=(b, nb),
        in_specs=[u_spec(0), u_spec(1), u_spec(2), w_spec(0), w_spec(1), w_spec(2)],
        out_specs=[out_spec, out_spec],
        out_shape=[jax.ShapeDtypeStruct((b, d, s), F32)] * 2,
        compiler_params=_params("parallel", "parallel"),
        name="hy_conv_gate",
    )(u_t, u_t, u_t, cwb, cwb, cwb)


def _hy_filter_kernel(w1t_ref, b1_ref, w2t_ref, b2_ref, w3t_ref, fr_ref, o_ref, *, length, lt, nfeat):
    half = pl.program_id(0)
    j = pl.program_id(1)
    d = o_ref.shape[0]
    rows = w1t_ref.shape[1]
    hi = lax.Precision.HIGHEST
    t = half * length + j * lt + lax.broadcasted_iota(I32, (rows, lt), 1)
    pos = jnp.where(half == 0, t, 2 * length - t).astype(F32)
    r = lax.broadcasted_iota(I32, (rows, lt), 0)
    tlin = pos / (length - 1.0)
    w = 2.0 * math.pi * pos / length
    fidx = jnp.where(r > HY_BANDS, r - (HY_BANDS + 1), r - 1).astype(F32)
    f = 1e-4 + fidx * ((HY_BANDS - 1 - 1e-4) / (HY_BANDS - 1))
    ang = f * w
    feat = jnp.where(r == 0, tlin, jnp.where(r <= HY_BANDS, jnp.cos(ang), -jnp.sin(ang)))
    feat = jnp.where(r < nfeat, feat, 0.0)
    fr = fr_ref[...]
    h = jnp.sin(fr * (jnp.dot(w1t_ref[...], feat, precision=hi, preferred_element_type=F32) + b1_ref[...]))
    h = jnp.sin(fr * (jnp.dot(w2t_ref[...], h, precision=hi, preferred_element_type=F32) + b2_ref[...]))
    out = jnp.dot(w3t_ref[...], h, precision=hi, preferred_element_type=F32)
    c = lax.broadcasted_iota(I32, (d, lt), 0).astype(F32)
    min_decay = math.log(HY_DECAY_TARGET) / HY_FAST_DECAY_PCT
    max_decay = math.log(HY_DECAY_TARGET) / HY_SLOW_DECAY_PCT
    delta = min_decay + c * ((max_decay - min_decay) / (d - 1.0))
    decay = jnp.exp(-tlin[0:1, :] * jnp.abs(delta))
    valid = pos[0:1, :] < length
    o_ref[...] = jnp.where(valid, out * decay, 0.0)


def _hy_filter(f_w1, f_b1, f_w2, f_b2, f_w3, f_freq, d, length):
    nfeat, hid = f_w1.shape
    rows = ((nfeat + 7) // 8) * 8
    w1t = jnp.zeros((hid, rows), F32).at[:, :nfeat].set(f_w1.T)
    lt = min(512, length)
    col = lambda a: a.reshape(hid, 1)
    small = lambda shape: pl.BlockSpec(shape, lambda h, j: (0, 0))
    return pl.pallas_call(
        functools.partial(_hy_filter_kernel, length=length, lt=lt, nfeat=nfeat),
        grid=(2, length // lt),
        in_specs=[small((hid, rows)), small((hid, 1)), small((hid, hid)), small((hid, 1)),
                  pl.BlockSpec((d, hid), lambda h, j: (h, 0)), small((hid, 1))],
        out_specs=pl.BlockSpec((d, lt), lambda h, j: (0, h * (length // lt) + j)),
        out_shape=jax.ShapeDtypeStruct((d, 2 * length), F32),
        compiler_params=_params("parallel", "parallel"),
        name="hy_filter",
    )(w1t, col(f_b1), f_w2.T, col(f_b2), f_w3.T, col(f_freq))


def _dft_tables(length):
    n = 2 * length
    n1 = n // LANES
    r = n1 // 2
    k1 = np.arange(n1)[:, None]
    ang1 = 2.0 * np.pi * k1 * np.arange(n1)[None, :] / n1
    c1, s1 = np.cos(ang1), np.sin(ang1)
    m1_real = np.concatenate([c1, -s1], axis=0)
    m1_cplx = np.block([[c1[:, :r], s1[:, :r]], [-s1[:, :r], c1[:, :r]]])
    m1_inv = np.block([[c1[:r, :], -s1[:r, :]], [s1[:r, :], c1[:r, :]]])
    ang_t = 2.0 * np.pi * k1 * np.arange(LANES)[None, :] / n
    tw = np.stack([np.cos(ang_t), np.sin(ang_t)])
    n2 = np.arange(LANES)
    ang2 = 2.0 * np.pi * n2[:, None] * n2[None, :] / LANES
    c2, s2 = np.cos(ang2), np.sin(ang2)
    w2_fwd = np.block([[c2, -s2], [s2, c2]])
    w2_inv = np.block([[c2, s2], [-s2, c2]])
    f = lambda a: jnp.asarray(a, F32)
    return dict(n1=n1, r=r, m1_real=f(m1_real), m1_cplx=f(m1_cplx), m1_inv=f(m1_inv),
                tw=f(tw), w2_fwd=f(w2_fwd), w2_inv=f(w2_inv))


def _wide(ref3, ct):
    return jnp.concatenate([ref3[c] for c in range(ct)], axis=1)


def _fft_fwd(x_wide, m1, tc, ts, w2, ct, n1, mm):
    a = mm(m1, x_wide)
    ar, ai = a[:n1], a[n1:]
    tall = []
    for c in range(ct):
        arc, aic = ar[:, c * LANES:(c + 1) * LANES], ai[:, c * LANES:(c + 1) * LANES]
        tall.append(jnp.concatenate([arc * tc + aic * ts, aic * tc - arc * ts], axis=1))
    return mm(jnp.concatenate(tall, axis=0), w2)


def _fft_inv(y_tall, w2i, tc, ts, m1i, ct, n1, mm):
    cc = mm(y_tall, w2i)
    dr, di = [], []
    for c in range(ct):
        cr, ci = cc[c * n1:(c + 1) * n1, :LANES], cc[c * n1:(c + 1) * n1, LANES:]
        dr.append(cr * tc - ci * ts)
        di.append(ci * tc + cr * ts)
    d_wide = jnp.concatenate([jnp.concatenate(dr, axis=1), jnp.concatenate(di, axis=1)], axis=0)
    return mm(m1i, d_wide)


def _hy_filter_fft_kernel(k_ref, m1_ref, tw_ref, w2_ref, kr_ref, ki_ref, *, ct, n1):
    x = _fft_fwd(_wide(k_ref, ct), m1_ref[...], tw_ref[0], tw_ref[1], w2_ref[...], ct, n1, _dot3)
    x = x * (1.0 / (n1 * LANES))
    kr_ref[...] = x[:, :LANES].reshape(ct, n1, LANES)
    ki_ref[...] = x[:, LANES:].reshape(ct, n1, LANES)


def _hy_filter_fft(k4, tabs, ct):
    d, n1, _ = k4.shape
    const = lambda a: pl.BlockSpec(a.shape, lambda i, nd=a.ndim: (0,) * nd)
    blk = pl.BlockSpec((ct, n1, LANES), lambda i: (i, 0, 0))
    return pl.pallas_call(
        functools.partial(_hy_filter_fft_kernel, ct=ct, n1=n1),
        grid=(d // ct,),
        in_specs=[blk, const(tabs["m1_real"]), const(tabs["tw"]), const(tabs["w2_fwd"])],
        out_specs=[blk, blk],
        out_shape=[jax.ShapeDtypeStruct((d, n1, LANES), F32)] * 2,
        compiler_params=_params("parallel"),
        name="hy_filter_fft",
    )(k4, tabs["m1_real"], tabs["tw"], tabs["w2_fwd"])


def _hy_longconv_kernel(z_ref, x0_ref, kr_ref, ki_ref, bias_ref, m1_ref, m1i_ref, tw_ref, w2_ref, w2i_ref,
                        o_ref, *, ct, n1, r):
    tc, ts = tw_ref[0], tw_ref[1]
    x_wide = jnp.concatenate([_wide(z_ref.at[0], ct), _wide(z_ref.at[1], ct)], axis=0)
    x = _fft_fwd(x_wide, m1_ref[...], tc, ts, w2_ref[...], ct, n1, _dot1)
    xr, xi = x[:, :LANES], x[:, LANES:]
    kr = kr_ref[...].reshape(ct * n1, LANES)
    ki = ki_ref[...].reshape(ct * n1, LANES)
    y_tall = jnp.concatenate([xr * kr - xi * ki, xr * ki + xi * kr], axis=1)
    y = _fft_inv(y_tall, w2i_ref[...], tc, ts, m1i_ref[...], ct, n1, _dot1)
    for p in range(2):
        for c in range(ct):
            conv = y[p * r:(p + 1) * r, c * LANES:(c + 1) * LANES]
            o_ref[p, c] = (conv + bias_ref[c] * z_ref[p, c]) * x0_ref[p, c]


def _hy_longconv(z4, x0c4, kr, ki, bias, tabs, ct):
    b, d, r, _ = z4.shape
    n1 = tabs["n1"]
    const = lambda a: pl.BlockSpec(a.shape, lambda i, j, nd=a.ndim: (0,) * nd)
    data = pl.BlockSpec((2, ct, r, LANES), lambda i, j: (j, i, 0, 0))
    spec = pl.BlockSpec((ct, n1, LANES), lambda i, j: (i, 0, 0))
    consts = [tabs["m1_cplx"], tabs["m1_inv"], tabs["tw"], tabs["w2_fwd"], tabs["w2_inv"]]
    return pl.pallas_call(
        functools.partial(_hy_longconv_kernel, ct=ct, n1=n1, r=r),
        grid=(d // ct, b // 2),
        in_specs=[data, data, spec, spec, pl.BlockSpec((ct, 1, 1), lambda i, j: (i, 0, 0))]
                 + [const(a) for a in consts],
        out_specs=data,
        out_shape=jax.ShapeDtypeStruct((b, d, r, LANES), F32),
        compiler_params=_params("parallel", "parallel"),
        name="hy_longconv",
    )(z4, x0c4, kr, ki, bias.reshape(d, 1, 1), *consts)


def _router_tail(x1, m, gf_ref, wrt_ref, x1_ref, hf_ref, aff_ref):
    x1_ref[0] = x1
    hf = _norm_mod(x1, gf_ref[...], m[3:4], m[4:5]).astype(BF16)
    hf_ref[0] = hf
    logits = _dot_nt(wrt_ref[...], hf)
    ex = jnp.exp(logits - jnp.max(logits, axis=0, keepdims=True))
    aff_ref[0] = ex / jnp.sum(ex, axis=0, keepdims=True)


def _hy_outproj_kernel(y_ref, x_ref, m_ref, w_ref, b_ref, gf_ref, wrt_ref, x1_ref, hf_ref, aff_ref):
    m = m_ref[0]
    y = jnp.transpose(y_ref[0]).astype(BF16)
    x1 = x_ref[0] + m[2:3] * (_dot(y, w_ref[...]) + b_ref[...])
    _router_tail(x1, m, gf_ref, wrt_ref, x1_ref, hf_ref, aff_ref)


def _mla_outproj_kernel(y_ref, x_ref, m_ref, w_ref, gf_ref, wrt_ref, x1_ref, hf_ref, aff_ref):
    m = m_ref[0]
    x1 = x_ref[0] + m[2:3] * _dot(y_ref[0].astype(BF16), w_ref[...])
    _router_tail(x1, m, gf_ref, wrt_ref, x1_ref, hf_ref, aff_ref)


def _outproj(y, x, mod, w_out, b_out, g_ffn, w_router_t, tt, channel_major):
    b, s, d = x.shape
    e = w_router_t.shape[0]
    dy = w_out.shape[0]
    tile = pl.BlockSpec((1, tt, d), lambda i, j: (i, j, 0))
    const = lambda shape: pl.BlockSpec(shape, lambda i, j: (0, 0))
    if channel_major:
        y_spec = pl.BlockSpec((1, dy, tt), lambda i, j: (i, 0, j))
        kern, extra, extra_specs = _hy_outproj_kernel, [b_out.reshape(1, d)], [const((1, d))]
    else:
        y_spec = pl.BlockSpec((1, tt, dy), lambda i, j: (i, j, 0))
        kern, extra, extra_specs = _mla_outproj_kernel, [], []
    return pl.pallas_call(
        kern,
        grid=(b, s // tt),
        in_specs=[y_spec, tile, pl.BlockSpec((1, 6, d), lambda i, j: (i, 0, 0)), const((dy, d))]
                 + extra_specs + [const((1, d)), const((e, d))],
        out_specs=[tile, tile, pl.BlockSpec((1, e, tt), lambda i, j: (i, 0, j))],
        out_shape=[jax.ShapeDtypeStruct((b, s, d), F32), jax.ShapeDtypeStruct((b, s, d), BF16),
                   jax.ShapeDtypeStruct((b, e, s), F32)],
        compiler_params=_params("parallel", "parallel"),
        name="outproj_router",
    )(y, x, mod, w_out, *extra, g_ffn.reshape(1, d), w_router_t)


def _route_kernel(a_ref, o_ref, *, n_exp, n_chunk, cap):
    rows = n_exp * n_chunk
    a = a_ref[0].reshape(rows, LANES)
    ri = lax.broadcasted_iota(I32, (rows, rows), 0)
    rj = lax.broadcasted_iota(I32, (rows, rows), 1)
    same = (ri // n_chunk) == (rj // n_chunk)
    blk_all = jnp.where(same, 1.0, 0.0).astype(BF16)
    blk_before = jnp.where(same & (rj < ri), 1.0, 0.0).astype(BF16)
    li = lax.broadcasted_iota(I32, (LANES, LANES), 0)
    lj = lax.broadcasted_iota(I32, (LANES, LANES), 1)
    ones = jnp.ones((LANES, LANES), BF16)
    before = jnp.where(li < lj, 1.0, 0.0).astype(BF16)

    def as_bf16(mask):
        return jnp.where(mask, 1.0, 0.0).astype(BF16)

    def expert_count(mask):
        return _dot(blk_all, _dot(as_bf16(mask), ones).astype(BF16))

    def prefix(mask):
        mb = as_bf16(mask)
        return _dot(mb, before) + _dot(blk_before, _dot(mb, ones).astype(BF16))

    cur = jnp.zeros((rows, LANES), I32)
    for bit in range(30, -1, -1):
        cand = cur | (1 << bit)
        cur = jnp.where(expert_count(a >= pltpu.bitcast(cand, F32)) >= cap, cand, cur)
    lo = pltpu.bitcast(cur, F32)
    hi = pltpu.bitcast(cur + 1, F32)
    for _ in range(16):
        mid = lo + (hi - lo) * 0.5
        take = expert_count(a >= mid) >= cap
        lo = jnp.where(take, mid, lo)
        hi = jnp.where(take, hi, mid)
    gt = a >= hi
    eq = (a >= lo) & (a < hi)
    need = cap - expert_count(gt)
    sel = gt | (eq & (prefix(eq) < need))
    slot = prefix(sel).astype(I32)
    o_ref[0] = jnp.where(sel, slot, -1).reshape(n_exp, n_chunk, LANES)


def _route(aff_t, cap):
    b, e, t = aff_t.shape
    nc = t // LANES
    blk = pl.BlockSpec((1, e, nc, LANES), lambda i: (i, 0, 0, 0))
    slot = pl.pallas_call(
        functools.partial(_route_kernel, n_exp=e, n_chunk=nc, cap=cap),
        grid=(b,),
        in_specs=[blk],
        out_specs=blk,
        out_shape=jax.ShapeDtypeStruct((b, e, nc, LANES), I32),
        compiler_params=_params("parallel"),
        name="route_topc",
    )(aff_t.reshape(b, e, nc, LANES))
    return slot.reshape(b, e, t)


def _expert_kernel(hf_ref, slot_ref, aff_ref, wg_ref, wu_ref, wd_ref, o_ref, *, cap, tk):
    t = hf_ref.shape[1]
    slot_id = lax.broadcasted_iota(I32, (cap, tk), 0)
    xs = jnp.zeros((cap, hf_ref.shape[2]), F32)
    gsl = jnp.zeros((cap, 1), F32)
    for k in range(t // tk):
        hit = slot_id == slot_ref[0, 0, :, k * tk:(k + 1) * tk]
        xs = xs + _dot(jnp.where(hit, 1.0, 0.0).astype(BF16), hf_ref[0, k * tk:(k + 1) * tk, :])
        gsl = gsl + jnp.sum(jnp.where(hit, aff_ref[0, 0, :, k * tk:(k + 1) * tk], 0.0), axis=1, keepdims=True)
    xs = xs.astype(BF16)
    gate = _dot(xs, wg_ref[0])
    up = _dot(xs, wu_ref[0])
    hid = gate / (1.0 + jnp.exp(-gate)) * up
    o_ref[0, 0] = _dot(hid.astype(BF16), wd_ref[0]) * gsl


def _experts(hf, slot_row, aff_row, wg, wu, wd, cap):
    b, t, d = hf.shape
    e, _, f = wg.shape
    tk = min(1024, t)
    row = pl.BlockSpec((1, 1, 1, t), lambda i, j: (i, j, 0, 0))
    return pl.pallas_call(
        functools.partial(_expert_kernel, cap=cap, tk=tk),
        grid=(b, e),
        in_specs=[pl.BlockSpec((1, t, d), lambda i, j: (i, 0, 0)), row, row,
                  pl.BlockSpec((1, d, f), lambda i, j: (j, 0, 0)),
                  pl.BlockSpec((1, d, f), lambda i, j: (j, 0, 0)),
                  pl.BlockSpec((1, f, d), lambda i, j: (j, 0, 0))],
        out_specs=pl.BlockSpec((1, 1, cap, d), lambda i, j: (i, j, 0, 0)),
        out_shape=jax.ShapeDtypeStruct((b, e, cap, d), F32),
        compiler_params=_params("parallel", "arbitrary"),
        name="expert_ffn",
    )(hf, slot_row, aff_row, wg, wu, wd)


def _combine_kernel(ye_ref, slot_ref, x_ref, m_ref, gn_ref, o_ref, *, cap, final_norm):
    e = pl.program_id(2)

    @pl.when(e == 0)
    def _():
        o_ref[...] = jnp.zeros_like(o_ref)

    tt = o_ref.shape[1]
    hit = slot_ref[0, 0] == lax.broadcasted_iota(I32, (tt, cap), 1)
    pt = jnp.where(hit, 1.0, 0.0).astype(BF16)
    hi, lo = _split(ye_ref[0, 0])
    o_ref[0] += _dot(pt, hi) + _dot(pt, lo)

    @pl.when(e == pl.num_programs(2) - 1)
    def _():
        x2 = x_ref[0] + m_ref[0][5:6] * o_ref[0]
        o_ref[0] = _rms(x2, gn_ref[...]) if final_norm else x2


def _combine(ye, slot_col, x1, mod, g_final, tt, final_norm):
    b, e, cap, d = ye.shape
    t = x1.shape[1]
    tile = pl.BlockSpec((1, tt, d), lambda i, j, k: (i, j, 0))
    return pl.pallas_call(
        functools.partial(_combine_kernel, cap=cap, final_norm=final_norm),
        grid=(b, t // tt, e),
        in_specs=[pl.BlockSpec((1, 1, cap, d), lambda i, j, k: (i, k, 0, 0)),
                  pl.BlockSpec((1, 1, tt, 1), lambda i, j, k: (i, k, j, 0)),
                  tile, pl.BlockSpec((1, 6, d), lambda i, j, k: (i, 0, 0)),
                  pl.BlockSpec((1, d), lambda i, j, k: (0, 0))],
        out_specs=tile,
        out_shape=jax.ShapeDtypeStruct((b, t, d), F32),
        compiler_params=_params("parallel", "parallel", "arbitrary"),
        name="moe_combine",
    )(ye, slot_col, x1, mod, g_final.reshape(1, d))


def _moe(hf, aff_t, x1, mod, wg, wu, wd, g_final, final_norm):
    b, t, d = hf.shape
    e = aff_t.shape[1]
    cap = EC_CAPACITY_FACTOR * t // e
    slot = _route(aff_t, cap)
    ye = _experts(hf, slot.reshape(b, e, 1, t), aff_t.reshape(b, e, 1, t), wg, wu, wd, cap)
    return _combine(ye, slot.reshape(b, e, t, 1), x1, mod, g_final, min(1024, t), final_norm)


def _rope(x, cos_t, sin_lo, sin_hi):
    return x * cos_t + pltpu.roll(x, LANES - QK_ROPE // 2, 1) * sin_lo + pltpu.roll(x, QK_ROPE // 2, 1) * sin_hi


def _mla_proj_kernel(x_ref, m_ref, g_ref, pos_ref, invf_ref, win_ref, qg_ref, wq_ref, kg_ref, wkn_ref, wv_ref,
                     q_ref, k_ref, v_ref):
    m = m_ref[0]
    h = _norm_mod(x_ref[0], g_ref[...], m[0:1], m[1:2]).astype(BF16)
    a = _dot(h, win_ref[...])
    cq = a[:, :Q_LORA]
    ckv = a[:, Q_LORA:Q_LORA + KV_LORA]
    kpe = a[:, Q_LORA + KV_LORA:]
    q = _dot(_rms(cq, qg_ref[...]).astype(BF16), wq_ref[...])
    kvn = _rms(ckv, kg_ref[...]).astype(BF16)
    kn = _dot(kvn, wkn_ref[...])
    v_ref[0] = _dot(kvn, wv_ref[...]).astype(BF16)
    ang = pos_ref[0] * invf_ref[...]
    lane = lax.broadcasted_iota(I32, ang.shape, 1)
    cos_t = jnp.where(lane < QK_ROPE, jnp.cos(ang), 0.0)
    sin_a = jnp.sin(ang)
    sin_lo = jnp.where(lane < QK_ROPE // 2, -sin_a, 0.0)
    sin_hi = jnp.where((lane >= QK_ROPE // 2) & (lane < QK_ROPE), sin_a, 0.0)
    kpe = _rope(kpe, cos_t, sin_lo, sin_hi)
    qs, ks = [], []
    for hd in range(MLA_HEADS):
        qs.append(q[:, hd * HEAD_PAD:hd * HEAD_PAD + QK_NOPE])
        qs.append(_rope(q[:, hd * HEAD_PAD + QK_NOPE:(hd + 1) * HEAD_PAD], cos_t, sin_lo, sin_hi))
        ks.append(kn[:, hd * QK_NOPE:(hd + 1) * QK_NOPE])
        ks.append(kpe)
    q_ref[0] = jnp.concatenate(qs, axis=1).astype(BF16)
    k_ref[0] = jnp.concatenate(ks, axis=1).astype(BF16)


def _mla_proj(x, mod, g, positions, w_in, q_g, w_qb, kv_g, w_kvb, tt):
    b, s, d = x.shape
    nh = MLA_HEADS
    pad = HEAD_PAD - QK_NOPE - QK_ROPE
    win = jnp.concatenate([w_in, jnp.zeros((d, LANES - QK_ROPE), F32)], axis=1).astype(BF16)
    wq = w_qb.reshape(Q_LORA, nh, QK_NOPE + QK_ROPE)
    wq = jnp.concatenate([wq, jnp.zeros((Q_LORA, nh, pad), F32)], axis=2).reshape(Q_LORA, nh * HEAD_PAD).astype(BF16)
    wkv = w_kvb.reshape(KV_LORA, nh, QK_NOPE + V_HEAD)
    wkn = wkv[:, :, :QK_NOPE].reshape(KV_LORA, nh * QK_NOPE).astype(BF16)
    wv = wkv[:, :, QK_NOPE:].reshape(KV_LORA, nh * V_HEAD).astype(BF16)
    inv_freq = ROPE_THETA ** (-jnp.arange(0, QK_ROPE, 2, dtype=F32) / QK_ROPE)
    invf = jnp.concatenate([inv_freq, inv_freq, jnp.zeros((LANES - QK_ROPE,), F32)]).reshape(1, LANES)
    pos = positions.astype(F32).reshape(b, s, 1)
    const = lambda a: pl.BlockSpec(a.shape, lambda i, j: (0, 0))
    tile = lambda w: pl.BlockSpec((1, tt, w), lambda i, j: (i, j, 0))
    g2, qg2, kg2 = g.reshape(1, d), q_g.reshape(1, Q_LORA), kv_g.reshape(1, KV_LORA)
    return pl.pallas_call(
        _mla_proj_kernel,
        grid=(b, s // tt),
        in_specs=[tile(d), pl.BlockSpec((1, 6, d), lambda i, j: (i, 0, 0)), const(g2), tile(1), const(invf),
                  const(win), const(qg2), const(wq), const(kg2), const(wkn), const(wv)],
        out_specs=[tile(nh * HEAD_PAD), tile(nh * HEAD_PAD), tile(nh * V_HEAD)],
        out_shape=[jax.ShapeDtypeStruct((b, s, nh * HEAD_PAD), BF16),
                   jax.ShapeDtypeStruct((b, s, nh * HEAD_PAD), BF16),
                   jax.ShapeDtypeStruct((b, s, nh * V_HEAD), BF16)],
        compiler_params=_params("parallel", "parallel"),
        name="mla_proj",
    )(x, mod, g2, pos, invf, win, qg2, wq, kg2, wkn, wv)


def _mla_attn_kernel(q_ref, k_ref, v_ref, o_ref, *, n_sub):
    sub = q_ref.shape[1] // n_sub
    for i in range(n_sub):
        s = _dot_nt(q_ref[0, i * sub:(i + 1) * sub, :], k_ref[0])
        p = jnp.exp2((s - jnp.max(s, axis=1, keepdims=True)) * (SOFTMAX_SCALE * math.log2(math.e)))
        o = _dot(p.astype(BF16), v_ref[0])
        o_ref[0, i * sub:(i + 1) * sub, :] = o / jnp.sum(p, axis=1, keepdims=True)


def _mla_attn(q, k, v, tq):
    b, s, _ = q.shape
    nh = MLA_HEADS
    return pl.pallas_call(
        functools.partial(_mla_attn_kernel, n_sub=max(1, tq // 256)),
        grid=(b, nh, s // tq),
        in_specs=[pl.BlockSpec((1, tq, HEAD_PAD), lambda i, h, j: (i, j, h)),
                  pl.BlockSpec((1, s, HEAD_PAD), lambda i, h, j: (i, 0, h)),
                  pl.BlockSpec((1, s, V_HEAD), lambda i, h, j: (i, 0, h))],
        out_specs=pl.BlockSpec((1, tq, V_HEAD), lambda i, h, j: (i, j, h)),
        out_shape=jax.ShapeDtypeStruct((b, s, nh * V_HEAD), F32),
        compiler_params=_params("parallel", "parallel", "parallel"),
        name="mla_attn",
    )(q, k, v)


def kernel(x, c, positions, ada_w, ada_b, norm_mix_g, norm_ffn_g, hy_w_in, hy_b_in, hy_conv_w, hy_conv_b, hy_f_w1, hy_f_b1, hy_f_w2, hy_f_b2, hy_f_w3, hy_f_freq, hy_f_bias, hy_w_out, hy_b_out, mla_w_in, mla_q_norm_g, mla_w_qb, mla_kv_norm_g, mla_w_kvb, mla_w_out, moe_w_router, moe_w_gate, moe_w_up, moe_w_down, final_norm_g):
    b, s, d = x.shape
    depth = ada_w.shape[0]
    assert s % LANES == 0 and b % 2 == 0 and depth == 2
    tt = min(512, s)
    mod = _ada(c, ada_w, ada_b).reshape(depth, b, 6, d)

    u_t = _hy_inproj(x, mod[0], norm_mix_g[0], hy_w_in[0].T.astype(BF16), hy_b_in[0], tt)
    cwb = jnp.concatenate([hy_conv_w[0].T, hy_conv_b[0][:, None]], axis=1)
    z_t, x0c_t = _hy_conv_gate(u_t, cwb, d, min(64, d))
    tabs = _dft_tables(s)
    n1, r = tabs["n1"], tabs["r"]
    k_t = _hy_filter(hy_f_w1[0], hy_f_b1[0], hy_f_w2[0], hy_f_b2[0], hy_f_w3[0], hy_f_freq[0], d, s)
    ct = min(16, d)
    kr, ki = _hy_filter_fft(k_t.reshape(d, n1, LANES), tabs, ct)
    yg = _hy_longconv(z_t.reshape(b, d, r, LANES), x0c_t.reshape(b, d, r, LANES), kr, ki,
                      hy_f_bias[0, 0], tabs, ct).reshape(b, d, s)
    x1, hf, aff_t = _outproj(yg, x, mod[0], hy_w_out[0].astype(BF16), hy_b_out[0], norm_ffn_g[0],
                             moe_w_router[0].T.astype(BF16), tt, True)
    x2 = _moe(hf, aff_t, x1, mod[0], moe_w_gate[0].astype(BF16), moe_w_up[0].astype(BF16),
              moe_w_down[0].astype(BF16), final_norm_g, False)

    q, k, v = _mla_proj(x2, mod[1], norm_mix_g[1], positions, mla_w_in[0], mla_q_norm_g[0], mla_w_qb[0],
                        mla_kv_norm_g[0], mla_w_kvb[0], tt)
    o = _mla_attn(q, k, v, min(1024, s))
    x3, hf, aff_t = _outproj(o, x2, mod[1], mla_w_out[0].astype(BF16), None, norm_ffn_g[1],
                             moe_w_router[1].T.astype(BF16), tt, False)
    return _moe(hf, aff_t, x3, mod[1], moe_w_gate[1].astype(BF16), moe_w_up[1].astype(BF16),
                moe_w_down[1].astype(BF16), final_norm_g, True)
```

```python
import functools
import math

import numpy as np
import jax
import jax.numpy as jnp
from jax import lax
from jax.experimental import pallas as pl
from jax.experimental.pallas import tpu as pltpu

F32 = jnp.float32
BF16 = jnp.bfloat16
I32 = jnp.int32

EPS = 1e-6
LANES = 128
VMEM_LIMIT = 56 * 1024 * 1024

HY_BANDS = 16
HY_FAST_DECAY_PCT = 0.3
HY_SLOW_DECAY_PCT = 1.5
HY_DECAY_TARGET = 1e-2

MLA_HEADS = 8
QK_NOPE = 128
QK_ROPE = 64
V_HEAD = 128
Q_LORA = 256
KV_LORA = 128
ROPE_THETA = 10000.0
SOFTMAX_SCALE = 1.0 / math.sqrt(QK_NOPE + QK_ROPE)
HEAD_PAD = 256

EC_CAPACITY_FACTOR = 2

NT_DIMS = (((1,), (1,)), ((), ()))


def _params(*sem):
    return pltpu.CompilerParams(dimension_semantics=sem, vmem_limit_bytes=VMEM_LIMIT)


def _dot(a, b):
    return jnp.dot(a, b, preferred_element_type=F32)


def _dot_nt(a, b):
    return lax.dot_general(a, b, NT_DIMS, preferred_element_type=F32)


def _split(a):
    hi = a.astype(BF16)
    lo = (a - hi.astype(F32)).astype(BF16)
    return hi, lo


def _dot3(a, b):
    ah, al = _split(a)
    bh, bl = _split(b)
    return _dot(ah, bh) + (_dot(ah, bl) + _dot(al, bh))


def _dot1(a, b):
    return _dot(a.astype(BF16), b.astype(BF16))


def _rms(x, g):
    return x * lax.rsqrt(jnp.mean(x * x, axis=-1, keepdims=True) + EPS) * g


def _norm_mod(x, g, shift, scale):
    return _rms(x, g) * (1.0 + scale) + shift


def _ada_kernel(c_ref, w_ref, b_ref, o_ref):
    c = c_ref[...]
    cs = c / (1.0 + jnp.exp(-c))
    o_ref[0] = _dot(cs.astype(BF16), w_ref[0].astype(BF16)) + b_ref[0]


def _ada(c, ada_w, ada_b):
    depth, d, n = ada_w.shape
    b = c.shape[0]
    tn = n // 4
    return pl.pallas_call(
        _ada_kernel,
        grid=(depth, n // tn),
        in_specs=[pl.BlockSpec((b, d), lambda i, j: (0, 0)),
                  pl.BlockSpec((1, d, tn), lambda i, j: (i, 0, j)),
                  pl.BlockSpec((1, 1, tn), lambda i, j: (i, 0, j))],
        out_specs=pl.BlockSpec((1, b, tn), lambda i, j: (i, 0, j)),
        out_shape=jax.ShapeDtypeStruct((depth, b, n), F32),
        compiler_params=_params("parallel", "parallel"),
        name="ada_mod",
    )(c, ada_w, ada_b.reshape(depth, 1, n))


def _hy_inproj_kernel(x_ref, m_ref, g_ref, wt_ref, b_ref, o_ref):
    m = m_ref[0]
    h = _norm_mod(x_ref[0], g_ref[...], m[0:1], m[1:2])
    o_ref[0] = _dot_nt(wt_ref[...], h.astype(BF16)) + b_ref[...]


def _hy_inproj(x, mod, g, w_in_t, b_in, tt):
    b, s, d = x.shape
    n = w_in_t.shape[0]
    return pl.pallas_call(
        _hy_inproj_kernel,
        grid=(b, s // tt),
        in_specs=[pl.BlockSpec((1, tt, d), lambda i, j: (i, j, 0)),
                  pl.BlockSpec((1, 6, d), lambda i, j: (i, 0, 0)),
                  pl.BlockSpec((1, d), lambda i, j: (0, 0)),
                  pl.BlockSpec((n, d), lambda i, j: (0, 0)),
                  pl.BlockSpec((n, 1), lambda i, j: (0, 0))],
        out_specs=pl.BlockSpec((1, n, tt), lambda i, j: (i, 0, j)),
        out_shape=jax.ShapeDtypeStruct((b, n, s), F32),
        compiler_params=_params("parallel", "parallel"),
        name="hy_inproj",
    )(x, mod, g.reshape(1, d), w_in_t, b_in.reshape(n, 1))


def _short_conv(u, w):
    length = u.shape[1]
    lane = lax.broadcasted_iota(I32, u.shape, 1)
    prev = jnp.where(lane == 0, 0.0, pltpu.roll(u, 1, 1))
    nxt = jnp.where(lane == length - 1, 0.0, pltpu.roll(u, length - 1, 1))
    return w[:, 0:1] * prev + w[:, 1:2] * u + w[:, 2:3] * nxt + w[:, 3:4]


def _hy_conv_gate_kernel(x0_ref, x1_ref, v_ref, w0_ref, w1_ref, wv_ref, z_ref, x0c_ref):
    z_ref[0] = _short_conv(v_ref[0], wv_ref[...]) * _short_conv(x1_ref[0], w1_ref[...])
    x0c_ref[0] = _short_conv(x0_ref[0], w0_ref[...])


def _hy_conv_gate(u_t, cwb, d, ct):
    b, _, s = u_t.shape
    nb = d // ct
    u_spec = lambda p: pl.BlockSpec((1, ct, s), lambda i, j, p=p: (i, p * nb + j, 0))
    w_spec = lambda p: pl.BlockSpec((ct, 4), lambda i, j, p=p: (p * nb + j, 0))
    out_spec = pl.BlockSpec((1, ct, s), lambda i, j: (i, j, 0))
    return pl.pallas_call(
        _hy_conv_gate_kernel,
        grid=(b, nb),
        in_specs=[u_spec(0), u_spec(1), u_spec(2), w_spec(0), w_spec(1), w_spec(2)],
        out_specs=[out_spec, out_spec],
        out_shape=[jax.ShapeDtypeStruct((b, d, s), F32)] * 2,
        compiler_params=_params("parallel", "parallel"),
        name="hy_conv_gate",
    )(u_t, u_t, u_t, cwb, cwb, cwb)


def _hy_filter_kernel(w1t_ref, b1_ref, w2t_ref, b2_ref, w3t_ref, fr_ref, o_ref, *, length, lt, nfeat):
    half = pl.program_id(0)
    j = pl.program_id(1)
    d = o_ref.shape[0]
    rows = w1t_ref.shape[1]
    hi = lax.Precision.HIGHEST
    t = half * length + j * lt + lax.broadcasted_iota(I32, (rows, lt), 1)
    pos = jnp.where(half == 0, t, 2 * length - t).astype(F32)
    r = lax.broadcasted_iota(I32, (rows, lt), 0)
    tlin = pos / (length - 1.0)
    w = 2.0 * math.pi * pos / length
    fidx = jnp.where(r > HY_BANDS, r - (HY_BANDS + 1), r - 1).astype(F32)
    f = 1e-4 + fidx * ((HY_BANDS - 1 - 1e-4) / (HY_BANDS - 1))
    ang = f * w
    feat = jnp.where(r == 0, tlin, jnp.where(r <= HY_BANDS, jnp.cos(ang), -jnp.sin(ang)))
    feat = jnp.where(r < nfeat, feat, 0.0)
    fr = fr_ref[...]
    h = jnp.sin(fr * (jnp.dot(w1t_ref[...], feat, precision=hi, preferred_element_type=F32) + b1_ref[...]))
    h = jnp.sin(fr * (jnp.dot(w2t_ref[...], h, precision=hi, preferred_element_type=F32) + b2_ref[...]))
    out = jnp.dot(w3t_ref[...], h, precision=hi, preferred_element_type=F32)
    c = lax.broadcasted_iota(I32, (d, lt), 0).astype(F32)
    min_decay = math.log(HY_DECAY_TARGET) / HY_FAST_DECAY_PCT
    max_decay = math.log(HY_DECAY_TARGET) / HY_SLOW_DECAY_PCT
    delta = min_decay + c * ((max_decay - min_decay) / (d - 1.0))
    decay = jnp.exp(-tlin[0:1, :] * jnp.abs(delta))
    valid = pos[0:1, :] < length
    o_ref[...] = jnp.where(valid, out * decay, 0.0)


def _hy_filter(f_w1, f_b1, f_w2, f_b2, f_w3, f_freq, d, length):
    nfeat, hid = f_w1.shape
    rows = ((nfeat + 7) // 8) * 8
    w1t = jnp.zeros((hid, rows), F32).at[:, :nfeat].set(f_w1.T)
    lt = min(512, length)
    col = lambda a: a.reshape(hid, 1)
    small = lambda shape: pl.BlockSpec(shape, lambda h, j: (0, 0))
    return pl.pallas_call(
        functools.partial(_hy_filter_kernel, length=length, lt=lt, nfeat=nfeat),
        grid=(2, length // lt),
        in_specs=[small((hid, rows)), small((hid, 1)), small((hid, hid)), small((hid, 1)),
                  pl.BlockSpec((d, hid), lambda h, j: (h, 0)), small((hid, 1))],
        out_specs=pl.BlockSpec((d, lt), lambda h, j: (0, h * (length // lt) + j)),
        out_shape=jax.ShapeDtypeStruct((d, 2 * length), F32),
        compiler_params=_params("parallel", "parallel"),
        name="hy_filter",
    )(w1t, col(f_b1), f_w2.T, col(f_b2), f_w3.T, col(f_freq))


def _dft_tables(length):
    n = 2 * length
    n1 = n // LANES
    r = n1 // 2
    k1 = np.arange(n1)[:, None]
    ang1 = 2.0 * np.pi * k1 * np.arange(n1)[None, :] / n1
    c1, s1 = np.cos(ang1), np.sin(ang1)
    m1_real = np.concatenate([c1, -s1], axis=0)
    m1_cplx = np.block([[c1[:, :r], s1[:, :r]], [-s1[:, :r], c1[:, :r]]])
    m1_inv = np.block([[c1[:r, :], -s1[:r, :]], [s1[:r, :], c1[:r, :]]])
    ang_t = 2.0 * np.pi * k1 * np.arange(LANES)[None, :] / n
    tw = np.stack([np.cos(ang_t), np.sin(ang_t)])
    n2 = np.arange(LANES)
    ang2 = 2.0 * np.pi * n2[:, None] * n2[None, :] / LANES
    c2, s2 = np.cos(ang2), np.sin(ang2)
    w2_fwd = np.block([[c2, -s2], [s2, c2]])
    w2_inv = np.block([[c2, s2], [-s2, c2]])
    f = lambda a: jnp.asarray(a, F32)
    return dict(n1=n1, r=r, m1_real=f(m1_real), m1_cplx=f(m1_cplx), m1_inv=f(m1_inv),
                tw=f(tw), w2_fwd=f(w2_fwd), w2_inv=f(w2_inv))


def _wide(ref3, ct):
    return jnp.concatenate([ref3[c] for c in range(ct)], axis=1)


def _fft_fwd(x_wide, m1, tc, ts, w2, ct, n1, mm):
    a = mm(m1, x_wide)
    ar, ai = a[:n1], a[n1:]
    tall = []
    for c in range(ct):
        arc, aic = ar[:, c * LANES:(c + 1) * LANES], ai[:, c * LANES:(c + 1) * LANES]
        tall.append(jnp.concatenate([arc * tc + aic * ts, aic * tc - arc * ts], axis=1))
    return mm(jnp.concatenate(tall, axis=0), w2)


def _fft_inv(y_tall, w2i, tc, ts, m1i, ct, n1, mm):
    cc = mm(y_tall, w2i)
    dr, di = [], []
    for c in range(ct):
        cr, ci = cc[c * n1:(c + 1) * n1, :LANES], cc[c * n1:(c + 1) * n1, LANES:]
        dr.append(cr * tc - ci * ts)
        di.append(ci * tc + cr * ts)
    d_wide = jnp.concatenate([jnp.concatenate(dr, axis=1), jnp.concatenate(di, axis=1)], axis=0)
    return mm(m1i, d_wide)


def _hy_filter_fft_kernel(k_ref, m1_ref, tw_ref, w2_ref, kr_ref, ki_ref, *, ct, n1):
    x = _fft_fwd(_wide(k_ref, ct), m1_ref[...], tw_ref[0], tw_ref[1], w2_ref[...], ct, n1, _dot3)
    x = x * (1.0 / (n1 * LANES))
    kr_ref[...] = x[:, :LANES].reshape(ct, n1, LANES)
    ki_ref[...] = x[:, LANES:].reshape(ct, n1, LANES)


def _hy_filter_fft(k4, tabs, ct):
    d, n1, _ = k4.shape
    const = lambda a: pl.BlockSpec(a.shape, lambda i, nd=a.ndim: (0,) * nd)
    blk = pl.BlockSpec((ct, n1, LANES), lambda i: (i, 0, 0))
    return pl.pallas_call(
        functools.partial(_hy_filter_fft_kernel, ct=ct, n1=n1),
        grid=(d // ct,),
        in_specs=[blk, const(tabs["m1_real"]), const(tabs["tw"]), const(tabs["w2_fwd"])],
        out_specs=[blk, blk],
        out_shape=[jax.ShapeDtypeStruct((d, n1, LANES), F32)] * 2,
        compiler_params=_params("parallel"),
        name="hy_filter_fft",
    )(k4, tabs["m1_real"], tabs["tw"], tabs["w2_fwd"])


def _hy_longconv_kernel(z_ref, x0_ref, kr_ref, ki_ref, bias_ref, m1_ref, m1i_ref, tw_ref, w2_ref, w2i_ref,
                        o_ref, *, ct, n1, r):
    tc, ts = tw_ref[0], tw_ref[1]
    x_wide = jnp.concatenate([_wide(z_ref.at[0], ct), _wide(z_ref.at[1], ct)], axis=0)
    x = _fft_fwd(x_wide, m1_ref[...], tc, ts, w2_ref[...], ct, n1, _dot1)
    xr, xi = x[:, :LANES], x[:, LANES:]
    kr = kr_ref[...].reshape(ct * n1, LANES)
    ki = ki_ref[...].reshape(ct * n1, LANES)
    y_tall = jnp.concatenate([xr * kr - xi * ki, xr * ki + xi * kr], axis=1)
    y = _fft_inv(y_tall, w2i_ref[...], tc, ts, m1i_ref[...], ct, n1, _dot1)
    for p in range(2):
        for c in range(ct):
            conv = y[p * r:(p + 1) * r, c * LANES:(c + 1) * LANES]
            o_ref[p, c] = (conv + bias_ref[c] * z_ref[p, c]) * x0_ref[p, c]


def _hy_longconv(z4, x0c4, kr, ki, bias, tabs, ct):
    b, d, r, _ = z4.shape
    n1 = tabs["n1"]
    const = lambda a: pl.BlockSpec(a.shape, lambda i, j, nd=a.ndim: (0,) * nd)
    data = pl.BlockSpec((2, ct, r, LANES), lambda i, j: (j, i, 0, 0))
    spec = pl.BlockSpec((ct, n1, LANES), lambda i, j: (i, 0, 0))
    consts = [tabs["m1_cplx"], tabs["m1_inv"], tabs["tw"], tabs["w2_fwd"], tabs["w2_inv"]]
    return pl.pallas_call(
        functools.partial(_hy_longconv_kernel, ct=ct, n1=n1, r=r),
        grid=(d // ct, b // 2),
        in_specs=[data, data, spec, spec, pl.BlockSpec((ct, 1, 1), lambda i, j: (i, 0, 0))]
                 + [const(a) for a in consts],
        out_specs=data,
        out_shape=jax.ShapeDtypeStruct((b, d, r, LANES), F32),
        compiler_params=_params("parallel", "parallel"),
        name="hy_longconv",
    )(z4, x0c4, kr, ki, bias.reshape(d, 1, 1), *consts)


def _router_tail(x1, m, gf_ref, wrt_ref, x1_ref, hf_ref, aff_ref):
    x1_ref[0] = x1
    hf = _norm_mod(x1, gf_ref[...], m[3:4], m[4:5]).astype(BF16)
    hf_ref[0] = hf
    logits = _dot_nt(wrt_ref[...], hf)
    ex = jnp.exp(logits - jnp.max(logits, axis=0, keepdims=True))
    aff_ref[0] = ex / jnp.sum(ex, axis=0, keepdims=True)


def _hy_outproj_kernel(y_ref, x_ref, m_ref, w_ref, b_ref, gf_ref, wrt_ref, x1_ref, hf_ref, aff_ref):
    m = m_ref[0]
    y = jnp.transpose(y_ref[0]).astype(BF16)
    x1 = x_ref[0] + m[2:3] * (_dot(y, w_ref[...]) + b_ref[...])
    _router_tail(x1, m, gf_ref, wrt_ref, x1_ref, hf_ref, aff_ref)


def _mla_outproj_kernel(y_ref, x_ref, m_ref, w_ref, gf_ref, wrt_ref, x1_ref, hf_ref, aff_ref):
    m = m_ref[0]
    x1 = x_ref[0] + m[2:3] * _dot(y_ref[0].astype(BF16), w_ref[...])
    _router_tail(x1, m, gf_ref, wrt_ref, x1_ref, hf_ref, aff_ref)


def _outproj(y, x, mod, w_out, b_out, g_ffn, w_router_t, tt, channel_major):
    b, s, d = x.shape
    e = w_router_t.shape[0]
    dy = w_out.shape[0]
    tile = pl.BlockSpec((1, tt, d), lambda i, j: (i, j, 0))
    const = lambda shape: pl.BlockSpec(shape, lambda i, j: (0, 0))
    if channel_major:
        y_spec = pl.BlockSpec((1, dy, tt), lambda i, j: (i, 0, j))
        kern, extra, extra_specs = _hy_outproj_kernel, [b_out.reshape(1, d)], [const((1, d))]
    else:
        y_spec = pl.BlockSpec((1, tt, dy), lambda i, j: (i, j, 0))
        kern, extra, extra_specs = _mla_outproj_kernel, [], []
    return pl.pallas_call(
        kern,
        grid=(b, s // tt),
        in_specs=[y_spec, tile, pl.BlockSpec((1, 6, d), lambda i, j: (i, 0, 0)), const((dy, d))]
                 + extra_specs + [const((1, d)), const((e, d))],
        out_specs=[tile, tile, pl.BlockSpec((1, e, tt), lambda i, j: (i, 0, j))],
        out_shape=[jax.ShapeDtypeStruct((b, s, d), F32), jax.ShapeDtypeStruct((b, s, d), BF16),
                   jax.ShapeDtypeStruct((b, e, s), F32)],
        compiler_params=_params("parallel", "parallel"),
        name="outproj_router",
    )(y, x, mod, w_out, *extra, g_ffn.reshape(1, d), w_router_t)


def _route_kernel(a_ref, o_ref, idx_ref, *, n_exp, n_chunk, cap):
    rows = n_exp * n_chunk
    a = a_ref[0].reshape(rows, LANES)
    ri = lax.broadcasted_iota(I32, (rows, rows), 0)
    rj = lax.broadcasted_iota(I32, (rows, rows), 1)
    same = (ri // n_chunk) == (rj // n_chunk)
    blk_all = jnp.where(same, 1.0, 0.0).astype(BF16)
    blk_before = jnp.where(same & (rj < ri), 1.0, 0.0).astype(BF16)
    li = lax.broadcasted_iota(I32, (LANES, LANES), 0)
    lj = lax.broadcasted_iota(I32, (LANES, LANES), 1)
    ones = jnp.ones((LANES, LANES), BF16)
    before = jnp.where(li < lj, 1.0, 0.0).astype(BF16)

    def as_bf16(mask):
        return jnp.where(mask, 1.0, 0.0).astype(BF16)

    def expert_count(mask):
        return _dot(blk_all, _dot(as_bf16(mask), ones).astype(BF16))

    def prefix(mask):
        mb = as_bf16(mask)
        return _dot(mb, before) + _dot(blk_before, _dot(mb, ones).astype(BF16))

    cur = jnp.zeros((rows, LANES), I32)
    for bit in range(30, -1, -1):
        cand = cur | (1 << bit)
        cur = jnp.where(expert_count(a >= pltpu.bitcast(cand, F32)) >= cap, cand, cur)
    lo = pltpu.bitcast(cur, F32)
    hi = pltpu.bitcast(cur + 1, F32)
    for _ in range(16):
        mid = lo + (hi - lo) * 0.5
        take = expert_count(a >= mid) >= cap
        lo = jnp.where(take, mid, lo)
        hi = jnp.where(take, hi, mid)
    gt = a >= hi
    eq = (a >= lo) & (a < hi)
    need = cap - expert_count(gt)
    sel = gt | (eq & (prefix(eq) < need))
    sel_b = as_bf16(sel)
    chunk_total = _dot(sel_b, ones)
    chunk_start = _dot(blk_before, chunk_total.astype(BF16))
    within = _dot(sel_b, before)
    o_ref[0] = jnp.where(sel, (chunk_start + within).astype(I32), -1).reshape(n_exp, n_chunk, LANES)

    rank = jnp.where(sel, within + 1.0, 0.0).astype(BF16)
    start_row = jnp.transpose(chunk_start)[0:1, :]
    end_row = jnp.transpose(chunk_start + chunk_total)[0:1, :]
    s_col = lax.broadcasted_iota(I32, (cap, rows), 0).astype(F32)
    row_id = lax.broadcasted_iota(I32, (cap, rows), 1)
    lane_f = lax.broadcasted_iota(I32, (cap, LANES), 1).astype(F32)
    for ex in range(n_exp):
        in_chunk = ((row_id // n_chunk) == ex) & (start_row <= s_col) & (s_col < end_row)
        rank_rows = _dot(jnp.where(in_chunk, 1.0, 0.0).astype(BF16), rank)
        start_s = jnp.sum(jnp.where(in_chunk, start_row, 0.0), axis=1, keepdims=True)
        chunk_s = jnp.sum(jnp.where(in_chunk, (row_id - ex * n_chunk).astype(F32), 0.0), axis=1, keepdims=True)
        match = rank_rows == (s_col[:, 0:1] + 1.0 - start_s)
        lane_s = jnp.sum(jnp.where(match, lane_f, 0.0), axis=1, keepdims=True)
        idx_ref[0, ex] = (chunk_s * LANES + lane_s).astype(I32)


def _route(aff_t, cap):
    b, e, t = aff_t.shape
    nc = t // LANES
    blk = pl.BlockSpec((1, e, nc, LANES), lambda i: (i, 0, 0, 0))
    slot, idx = pl.pallas_call(
        functools.partial(_route_kernel, n_exp=e, n_chunk=nc, cap=cap),
        grid=(b,),
        in_specs=[blk],
        out_specs=[blk, pl.BlockSpec((1, e, cap, 1), lambda i: (i, 0, 0, 0))],
        out_shape=[jax.ShapeDtypeStruct((b, e, nc, LANES), I32), jax.ShapeDtypeStruct((b, e, cap, 1), I32)],
        compiler_params=_params("parallel"),
        name="route_topc",
    )(aff_t.reshape(b, e, nc, LANES))
    return slot.reshape(b, e, t), idx.reshape(b, e, cap)


SCATTER_UNROLL = 8


def _expert_kernel(idx_ref, hf_ref, slot_ref, aff_ref, wg_ref, wu_ref, wd_ref, acc_ref, ye_ref, *, cap, tk):
    t, d = hf_ref.shape[1], hf_ref.shape[2]
    nc = d // LANES

    @pl.when(pl.program_id(1) == 0)
    def _():
        acc_ref[...] = jnp.zeros_like(acc_ref)

    slot_id = lax.broadcasted_iota(I32, (cap, tk), 0)
    xs = jnp.zeros((cap, d), F32)
    gsl = jnp.zeros((cap, 1), F32)
    for k in range(t // tk):
        hit = slot_id == slot_ref[0, 0, :, k * tk:(k + 1) * tk]
        xs = xs + _dot(jnp.where(hit, 1.0, 0.0).astype(BF16), hf_ref[0, k * tk:(k + 1) * tk, :])
        gsl = gsl + jnp.sum(jnp.where(hit, aff_ref[0, 0, :, k * tk:(k + 1) * tk], 0.0), axis=1, keepdims=True)
    xs = xs.astype(BF16)
    gate = _dot(xs, wg_ref[0])
    up = _dot(xs, wu_ref[0])
    hid = gate / (1.0 + jnp.exp(-gate)) * up
    ye = _dot(hid.astype(BF16), wd_ref[0]) * gsl
    for c in range(nc):
        ye_ref[pl.ds(c, cap, stride=nc), :] = ye[:, c * LANES:(c + 1) * LANES]

    def tile_of(i):
        return pl.ds(pl.multiple_of(i * nc, nc), nc)

    def scatter_group(g, carry):
        base = g * SCATTER_UNROLL
        toks = [idx_ref[0, 0, 0, base + u] for u in range(SCATTER_UNROLL)]
        sums = [acc_ref[0, tile_of(toks[u]), :] + ye_ref[tile_of(base + u), :] for u in range(SCATTER_UNROLL)]
        for u in range(SCATTER_UNROLL):
            acc_ref[0, tile_of(toks[u]), :] = sums[u]
        return carry

    lax.fori_loop(0, cap // SCATTER_UNROLL, scatter_group, 0)


def _experts(hf, idx, slot_row, aff_row, wg, wu, wd, cap):
    b, t, d = hf.shape
    e, _, f = wg.shape
    nc = d // LANES
    tk = min(1024, t)
    row = pl.BlockSpec((1, 1, 1, t), lambda i, j: (i, j, 0, 0))
    once = pl.Buffered(1)
    return pl.pallas_call(
        functools.partial(_expert_kernel, cap=cap, tk=tk),
        grid=(b, e),
        in_specs=[pl.BlockSpec((1, 1, 1, cap), lambda i, j: (i, j, 0, 0), memory_space=pltpu.SMEM),
                  pl.BlockSpec((1, t, d), lambda i, j: (i, 0, 0), pipeline_mode=once), row, row,
                  pl.BlockSpec((1, d, f), lambda i, j: (j, 0, 0)),
                  pl.BlockSpec((1, d, f), lambda i, j: (j, 0, 0)),
                  pl.BlockSpec((1, f, d), lambda i, j: (j, 0, 0))],
        out_specs=pl.BlockSpec((1, t * nc, LANES), lambda i, j: (i, 0, 0), pipeline_mode=once),
        out_shape=jax.ShapeDtypeStruct((b, t * nc, LANES), F32),
        scratch_shapes=[pltpu.VMEM((cap * nc, LANES), F32)],
        compiler_params=_params("parallel", "arbitrary"),
        name="expert_ffn",
    )(idx.reshape(b, e, 1, cap), hf, slot_row, aff_row, wg, wu, wd)


def _combine_kernel(acc_ref, x_ref, m_ref, gn_ref, o_ref, *, final_norm):
    tt, d = o_ref.shape[1], o_ref.shape[2]
    nc = d // LANES
    moe = jnp.concatenate([acc_ref[0, pl.ds(c, tt, stride=nc), :] for c in range(nc)], axis=1)
    x2 = x_ref[0] + m_ref[0][5:6] * moe
    o_ref[0] = _rms(x2, gn_ref[...]) if final_norm else x2


def _combine(acc, x1, mod, g_final, tt, final_norm):
    b, t, d = x1.shape
    nc = d // LANES
    tile = pl.BlockSpec((1, tt, d), lambda i, j: (i, j, 0))
    return pl.pallas_call(
        functools.partial(_combine_kernel, final_norm=final_norm),
        grid=(b, t // tt),
        in_specs=[pl.BlockSpec((1, tt * nc, LANES), lambda i, j: (i, j, 0)),
                  tile, pl.BlockSpec((1, 6, d), lambda i, j: (i, 0, 0)),
                  pl.BlockSpec((1, d), lambda i, j: (0, 0))],
        out_specs=tile,
        out_shape=jax.ShapeDtypeStruct((b, t, d), F32),
        compiler_params=_params("parallel", "parallel"),
        name="moe_combine",
    )(acc, x1, mod, g_final.reshape(1, d))


def _moe(hf, aff_t, x1, mod, wg, wu, wd, g_final, final_norm):
    b, t, d = hf.shape
    e = aff_t.shape[1]
    cap = EC_CAPACITY_FACTOR * t // e
    slot, idx = _route(aff_t, cap)
    acc = _experts(hf, idx, slot.reshape(b, e, 1, t), aff_t.reshape(b, e, 1, t), wg, wu, wd, cap)
    return _combine(acc, x1, mod, g_final, min(512, t), final_norm)


def _rope(x, cos_t, sin_lo, sin_hi):
    return x * cos_t + pltpu.roll(x, LANES - QK_ROPE // 2, 1) * sin_lo + pltpu.roll(x, QK_ROPE // 2, 1) * sin_hi


def _mla_proj_kernel(x_ref, m_ref, g_ref, pos_ref, invf_ref, win_ref, qg_ref, wq_ref, kg_ref, wkn_ref, wv_ref,
                     q_ref, k_ref, v_ref):
    m = m_ref[0]
    h = _norm_mod(x_ref[0], g_ref[...], m[0:1], m[1:2]).astype(BF16)
    a = _dot(h, win_ref[...])
    cq = a[:, :Q_LORA]
    ckv = a[:, Q_LORA:Q_LORA + KV_LORA]
    kpe = a[:, Q_LORA + KV_LORA:]
    q = _dot(_rms(cq, qg_ref[...]).astype(BF16), wq_ref[...])
    kvn = _rms(ckv, kg_ref[...]).astype(BF16)
    kn = _dot(kvn, wkn_ref[...])
    v_ref[0] = _dot(kvn, wv_ref[...]).astype(BF16)
    ang = pos_ref[0] * invf_ref[...]
    lane = lax.broadcasted_iota(I32, ang.shape, 1)
    cos_t = jnp.where(lane < QK_ROPE, jnp.cos(ang), 0.0)
    sin_a = jnp.sin(ang)
    sin_lo = jnp.where(lane < QK_ROPE // 2, -sin_a, 0.0)
    sin_hi = jnp.where((lane >= QK_ROPE // 2) & (lane < QK_ROPE), sin_a, 0.0)
    kpe = _rope(kpe, cos_t, sin_lo, sin_hi)
    qs, ks = [], []
    for hd in range(MLA_HEADS):
        qs.append(q[:, hd * HEAD_PAD:hd * HEAD_PAD + QK_NOPE])
        qs.append(_rope(q[:, hd * HEAD_PAD + QK_NOPE:(hd + 1) * HEAD_PAD], cos_t, sin_lo, sin_hi))
        ks.append(kn[:, hd * QK_NOPE:(hd + 1) * QK_NOPE])
        ks.append(kpe)
    q_ref[0] = jnp.concatenate(qs, axis=1).astype(BF16)
    k_ref[0] = jnp.concatenate(ks, axis=1).astype(BF16)


def _mla_proj(x, mod, g, positions, w_in, q_g, w_qb, kv_g, w_kvb, tt):
    b, s, d = x.shape
    nh = MLA_HEADS
    pad = HEAD_PAD - QK_NOPE - QK_ROPE
    win = jnp.concatenate([w_in, jnp.zeros((d, LANES - QK_ROPE), F32)], axis=1).astype(BF16)
    wq = w_qb.reshape(Q_LORA, nh, QK_NOPE + QK_ROPE)
    wq = jnp.concatenate([wq, jnp.zeros((Q_LORA, nh, pad), F32)], axis=2).reshape(Q_LORA, nh * HEAD_PAD).astype(BF16)
    wkv = w_kvb.reshape(KV_LORA, nh, QK_NOPE + V_HEAD)
    wkn = wkv[:, :, :QK_NOPE].reshape(KV_LORA, nh * QK_NOPE).astype(BF16)
    wv = wkv[:, :, QK_NOPE:].reshape(KV_LORA, nh * V_HEAD).astype(BF16)
    inv_freq = ROPE_THETA ** (-jnp.arange(0, QK_ROPE, 2, dtype=F32) / QK_ROPE)
    invf = jnp.concatenate([inv_freq, inv_freq, jnp.zeros((LANES - QK_ROPE,), F32)]).reshape(1, LANES)
    pos = positions.astype(F32).reshape(b, s, 1)
    const = lambda a: pl.BlockSpec(a.shape, lambda i, j: (0, 0))
    tile = lambda w: pl.BlockSpec((1, tt, w), lambda i, j: (i, j, 0))
    g2, qg2, kg2 = g.reshape(1, d), q_g.reshape(1, Q_LORA), kv_g.reshape(1, KV_LORA)
    return pl.pallas_call(
        _mla_proj_kernel,
        grid=(b, s // tt),
        in_specs=[tile(d), pl.BlockSpec((1, 6, d), lambda i, j: (i, 0, 0)), const(g2), tile(1), const(invf),
                  const(win), const(qg2), const(wq), const(kg2), const(wkn), const(wv)],
        out_specs=[tile(nh * HEAD_PAD), tile(nh * HEAD_PAD), tile(nh * V_HEAD)],
        out_shape=[jax.ShapeDtypeStruct((b, s, nh * HEAD_PAD), BF16),
                   jax.ShapeDtypeStruct((b, s, nh * HEAD_PAD), BF16),
                   jax.ShapeDtypeStruct((b, s, nh * V_HEAD), BF16)],
        compiler_params=_params("parallel", "parallel"),
        name="mla_proj",
    )(x, mod, g2, pos, invf, win, qg2, wq, kg2, wkn, wv)


def _mla_attn_kernel(q_ref, k_ref, v_ref, o_ref, *, n_sub):
    sub = q_ref.shape[1] // n_sub
    for i in range(n_sub):
        s = _dot_nt(q_ref[0, i * sub:(i + 1) * sub, :], k_ref[0])
        p = jnp.exp2((s - jnp.max(s, axis=1, keepdims=True)) * (SOFTMAX_SCALE * math.log2(math.e)))
        o = _dot(p.astype(BF16), v_ref[0])
        o_ref[0, i * sub:(i + 1) * sub, :] = o / jnp.sum(p, axis=1, keepdims=True)


def _mla_attn(q, k, v, tq):
    b, s, _ = q.shape
    nh = MLA_HEADS
    return pl.pallas_call(
        functools.partial(_mla_attn_kernel, n_sub=max(1, tq // 256)),
        grid=(b, nh, s // tq),
        in_specs=[pl.BlockSpec((1, tq, HEAD_PAD), lambda i, h, j: (i, j, h)),
                  pl.BlockSpec((1, s, HEAD_PAD), lambda i, h, j: (i, 0, h)),
                  pl.BlockSpec((1, s, V_HEAD), lambda i, h, j: (i, 0, h))],
        out_specs=pl.BlockSpec((1, tq, V_HEAD), lambda i, h, j: (i, j, h)),
        out_shape=jax.ShapeDtypeStruct((b, s, nh * V_HEAD), F32),
        compiler_params=_params("parallel", "parallel", "parallel"),
        name="mla_attn",
    )(q, k, v)


def kernel(x, c, positions, ada_w, ada_b, norm_mix_g, norm_ffn_g, hy_w_in, hy_b_in, hy_conv_w, hy_conv_b, hy_f_w1, hy_f_b1, hy_f_w2, hy_f_b2, hy_f_w3, hy_f_freq, hy_f_bias, hy_w_out, hy_b_out, mla_w_in, mla_q_norm_g, mla_w_qb, mla_kv_norm_g, mla_w_kvb, mla_w_out, moe_w_router, moe_w_gate, moe_w_up, moe_w_down, final_norm_g):
    b, s, d = x.shape
    depth = ada_w.shape[0]
    assert s % LANES == 0 and b % 2 == 0 and depth == 2
    tt = min(512, s)
    mod = _ada(c, ada_w, ada_b).reshape(depth, b, 6, d)

    u_t = _hy_inproj(x, mod[0], norm_mix_g[0], hy_w_in[0].T.astype(BF16), hy_b_in[0], tt)
    cwb = jnp.concatenate([hy_conv_w[0].T, hy_conv_b[0][:, None]], axis=1)
    z_t, x0c_t = _hy_conv_gate(u_t, cwb, d, min(64, d))
    tabs = _dft_tables(s)
    n1, r = tabs["n1"], tabs["r"]
    k_t = _hy_filter(hy_f_w1[0], hy_f_b1[0], hy_f_w2[0], hy_f_b2[0], hy_f_w3[0], hy_f_freq[0], d, s)
    ct = min(16, d)
    kr, ki = _hy_filter_fft(k_t.reshape(d, n1, LANES), tabs, ct)
    yg = _hy_longconv(z_t.reshape(b, d, r, LANES), x0c_t.reshape(b, d, r, LANES), kr, ki,
                      hy_f_bias[0, 0], tabs, ct).reshape(b, d, s)
    x1, hf, aff_t = _outproj(yg, x, mod[0], hy_w_out[0].astype(BF16), hy_b_out[0], norm_ffn_g[0],
                             moe_w_router[0].T.astype(BF16), tt, True)
    x2 = _moe(hf, aff_t, x1, mod[0], moe_w_gate[0].astype(BF16), moe_w_up[0].astype(BF16),
              moe_w_down[0].astype(BF16), final_norm_g, False)

    q, k, v = _mla_proj(x2, mod[1], norm_mix_g[1], positions, mla_w_in[0], mla_q_norm_g[0], mla_w_qb[0],
                        mla_kv_norm_g[0], mla_w_kvb[0], tt)
    o = _mla_attn(q, k, v, min(1024, s))
    x3, hf, aff_t = _outproj(o, x2, mod[1], mla_w_out[0].astype(BF16), None, norm_ffn_g[1],
                             moe_w_router[1].T.astype(BF16), tt, False)
    return _moe(hf, aff_t, x3, mod[1], moe_w_gate[1].astype(BF16), moe_w_up[1].astype(BF16),
                moe_w_down[1].astype(BF16), final_norm_g, True)
```

```python
import functools
import math

import numpy as np
import jax
import jax.numpy as jnp
from jax import lax
from jax.experimental import pallas as pl
from jax.experimental.pallas import tpu as pltpu

F32 = jnp.float32
BF16 = jnp.bfloat16
I32 = jnp.int32

EPS = 1e-6
LANES = 128
VMEM_LIMIT = 56 * 1024 * 1024

HY_BANDS = 16
HY_FAST_DECAY_PCT = 0.3
HY_SLOW_DECAY_PCT = 1.5
HY_DECAY_TARGET = 1e-2

MLA_HEADS = 8
QK_NOPE = 128
QK_ROPE = 64
V_HEAD = 128
Q_LORA = 256
KV_LORA = 128
ROPE_THETA = 10000.0
SOFTMAX_SCALE = 1.0 / math.sqrt(QK_NOPE + QK_ROPE)
HEAD_PAD = 256

EC_CAPACITY_FACTOR = 2

NT_DIMS = (((1,), (1,)), ((), ()))


def _params(*sem):
    return pltpu.CompilerParams(dimension_semantics=sem, vmem_limit_bytes=VMEM_LIMIT)


def _dot(a, b):
    return jnp.dot(a, b, preferred_element_type=F32)


def _dot_nt(a, b):
    return lax.dot_general(a, b, NT_DIMS, preferred_element_type=F32)


def _split(a):
    hi = a.astype(BF16)
    lo = (a - hi.astype(F32)).astype(BF16)
    return hi, lo


def _dot3(a, b):
    ah, al = _split(a)
    bh, bl = _split(b)
    return _dot(ah, bh) + (_dot(ah, bl) + _dot(al, bh))


def _dot1(a, b):
    return _dot(a.astype(BF16), b.astype(BF16))


def _rms(x, g):
    return x * lax.rsqrt(jnp.mean(x * x, axis=-1, keepdims=True) + EPS) * g


def _norm_mod(x, g, shift, scale):
    return _rms(x, g) * (1.0 + scale) + shift


def _ada_kernel(c_ref, w_ref, b_ref, o_ref):
    c = c_ref[...]
    cs = c / (1.0 + jnp.exp(-c))
    o_ref[0] = _dot(cs.astype(BF16), w_ref[0].astype(BF16)) + b_ref[0]


def _ada(c, ada_w, ada_b):
    depth, d, n = ada_w.shape
    b = c.shape[0]
    tn = n // 4
    return pl.pallas_call(
        _ada_kernel,
        grid=(depth, n // tn),
        in_specs=[pl.BlockSpec((b, d), lambda i, j: (0, 0)),
                  pl.BlockSpec((1, d, tn), lambda i, j: (i, 0, j)),
                  pl.BlockSpec((1, 1, tn), lambda i, j: (i, 0, j))],
        out_specs=pl.BlockSpec((1, b, tn), lambda i, j: (i, 0, j)),
        out_shape=jax.ShapeDtypeStruct((depth, b, n), F32),
        compiler_params=_params("parallel", "parallel"),
        name="ada_mod",
    )(c, ada_w, ada_b.reshape(depth, 1, n))


def _hy_inproj_kernel(x_ref, m_ref, g_ref, wt_ref, b_ref, o_ref):
    m = m_ref[0]
    h = _norm_mod(x_ref[0], g_ref[...], m[0:1], m[1:2])
    o_ref[0] = _dot_nt(wt_ref[...], h.astype(BF16)) + b_ref[...]


def _hy_inproj(x, mod, g, w_in_t, b_in, tt):
    b, s, d = x.shape
    n = w_in_t.shape[0]
    return pl.pallas_call(
        _hy_inproj_kernel,
        grid=(b, s // tt),
        in_specs=[pl.BlockSpec((1, tt, d), lambda i, j: (i, j, 0)),
                  pl.BlockSpec((1, 6, d), lambda i, j: (i, 0, 0)),
                  pl.BlockSpec((1, d), lambda i, j: (0, 0)),
                  pl.BlockSpec((n, d), lambda i, j: (0, 0)),
                  pl.BlockSpec((n, 1), lambda i, j: (0, 0))],
        out_specs=pl.BlockSpec((1, n, tt), lambda i, j: (i, 0, j)),
        out_shape=jax.ShapeDtypeStruct((b, n, s), F32),
        compiler_params=_params("parallel", "parallel"),
        name="hy_inproj",
    )(x, mod, g.reshape(1, d), w_in_t, b_in.reshape(n, 1))


def _short_conv(u, w):
    length = u.shape[1]
    lane = lax.broadcasted_iota(I32, u.shape, 1)
    prev = jnp.where(lane == 0, 0.0, pltpu.roll(u, 1, 1))
    nxt = jnp.where(lane == length - 1, 0.0, pltpu.roll(u, length - 1, 1))
    return w[:, 0:1] * prev + w[:, 1:2] * u + w[:, 2:3] * nxt + w[:, 3:4]


def _hy_conv_gate_kernel(x0_ref, x1_ref, v_ref, w0_ref, w1_ref, wv_ref, z_ref, x0c_ref):
    z_ref[0] = _short_conv(v_ref[0], wv_ref[...]) * _short_conv(x1_ref[0], w1_ref[...])
    x0c_ref[0] = _short_conv(x0_ref[0], w0_ref[...])


def _hy_conv_gate(u_t, cwb, d, ct):
    b, _, s = u_t.shape
    nb = d // ct
    u_spec = lambda p: pl.BlockSpec((1, ct, s), lambda i, j, p=p: (i, p * nb + j, 0))
    w_spec = lambda p: pl.BlockSpec((ct, 4), lambda i, j, p=p: (p * nb + j, 0))
    out_spec = pl.BlockSpec((1, ct, s), lambda i, j: (i, j, 0))
    return pl.pallas_call(
        _hy_conv_gate_kernel,
        grid=(b, nb),
        in_specs=[u_spec(0), u_spec(1), u_spec(2), w_spec(0), w_spec(1), w_spec(2)],
        out_specs=[out_spec, out_spec],
        out_shape=[jax.ShapeDtypeStruct((b, d, s), F32)] * 2,
        compiler_params=_params("parallel", "parallel"),
        name="hy_conv_gate",
    )(u_t, u_t, u_t, cwb, cwb, cwb)


def _hy_filter_kernel(w1t_ref, b1_ref, w2t_ref, b2_ref, w3t_ref, fr_ref, o_ref, *, length, lt, nfeat):
    half = pl.program_id(0)
    j = pl.program_id(1)
    d = o_ref.shape[0]
    rows = w1t_ref.shape[1]
    hi = lax.Precision.HIGHEST
    t = half * length + j * lt + lax.broadcasted_iota(I32, (rows, lt), 1)
    pos = jnp.where(half == 0, t, 2 * length - t).astype(F32)
    r = lax.broadcasted_iota(I32, (rows, lt), 0)
    tlin = pos / (length - 1.0)
    w = 2.0 * math.pi * pos / length
    fidx = jnp.where(r > HY_BANDS, r - (HY_BANDS + 1), r - 1).astype(F32)
    f = 1e-4 + fidx * ((HY_BANDS - 1 - 1e-4) / (HY_BANDS - 1))
    ang = f * w
    feat = jnp.where(r == 0, tlin, jnp.where(r <= HY_BANDS, jnp.cos(ang), -jnp.sin(ang)))
    feat = jnp.where(r < nfeat, feat, 0.0)
    fr = fr_ref[...]
    h = jnp.sin(fr * (jnp.dot(w1t_ref[...], feat, precision=hi, preferred_element_type=F32) + b1_ref[...]))
    h = jnp.sin(fr * (jnp.dot(w2t_ref[...], h, precision=hi, preferred_element_type=F32) + b2_ref[...]))
    out = jnp.dot(w3t_ref[...], h, precision=hi, preferred_element_type=F32)
    c = lax.broadcasted_iota(I32, (d, lt), 0).astype(F32)
    min_decay = math.log(HY_DECAY_TARGET) / HY_FAST_DECAY_PCT
    max_decay = math.log(HY_DECAY_TARGET) / HY_SLOW_DECAY_PCT
    delta = min_decay + c * ((max_decay - min_decay) / (d - 1.0))
    decay = jnp.exp(-tlin[0:1, :] * jnp.abs(delta))
    valid = pos[0:1, :] < length
    o_ref[...] = jnp.where(valid, out * decay, 0.0)


def _hy_filter(f_w1, f_b1, f_w2, f_b2, f_w3, f_freq, d, length):
    nfeat, hid = f_w1.shape
    rows = ((nfeat + 7) // 8) * 8
    w1t = jnp.zeros((hid, rows), F32).at[:, :nfeat].set(f_w1.T)
    lt = min(512, length)
    col = lambda a: a.reshape(hid, 1)
    small = lambda shape: pl.BlockSpec(shape, lambda h, j: (0, 0))
    return pl.pallas_call(
        functools.partial(_hy_filter_kernel, length=length, lt=lt, nfeat=nfeat),
        grid=(2, length // lt),
        in_specs=[small((hid, rows)), small((hid, 1)), small((hid, hid)), small((hid, 1)),
                  pl.BlockSpec((d, hid), lambda h, j: (h, 0)), small((hid, 1))],
        out_specs=pl.BlockSpec((d, lt), lambda h, j: (0, h * (length // lt) + j)),
        out_shape=jax.ShapeDtypeStruct((d, 2 * length), F32),
        compiler_params=_params("parallel", "parallel"),
        name="hy_filter",
    )(w1t, col(f_b1), f_w2.T, col(f_b2), f_w3.T, col(f_freq))


def _dft_tables(length):
    n = 2 * length
    n1 = n // LANES
    r = n1 // 2
    k1 = np.arange(n1)[:, None]
    ang1 = 2.0 * np.pi * k1 * np.arange(n1)[None, :] / n1
    c1, s1 = np.cos(ang1), np.sin(ang1)
    m1_real = np.concatenate([c1, -s1], axis=0)
    m1_cplx = np.block([[c1[:, :r], s1[:, :r]], [-s1[:, :r], c1[:, :r]]])
    m1_inv = np.block([[c1[:r, :], -s1[:r, :]], [s1[:r, :], c1[:r, :]]])
    ang_t = 2.0 * np.pi * k1 * np.arange(LANES)[None, :] / n
    tw = np.stack([np.cos(ang_t), np.sin(ang_t)])
    n2 = np.arange(LANES)
    ang2 = 2.0 * np.pi * n2[:, None] * n2[None, :] / LANES
    c2, s2 = np.cos(ang2), np.sin(ang2)
    w2_fwd = np.block([[c2, -s2], [s2, c2]])
    w2_inv = np.block([[c2, s2], [-s2, c2]])
    f = lambda a: jnp.asarray(a, F32)
    return dict(n1=n1, r=r, m1_real=f(m1_real), m1_cplx=f(m1_cplx), m1_inv=f(m1_inv),
                tw=f(tw), w2_fwd=f(w2_fwd), w2_inv=f(w2_inv))


def _wide(ref3, ct):
    return jnp.concatenate([ref3[c] for c in range(ct)], axis=1)


def _fft_fwd(x_wide, m1, tc, ts, w2, ct, n1, mm):
    a = mm(m1, x_wide)
    ar, ai = a[:n1], a[n1:]
    tall = []
    for c in range(ct):
        arc, aic = ar[:, c * LANES:(c + 1) * LANES], ai[:, c * LANES:(c + 1) * LANES]
        tall.append(jnp.concatenate([arc * tc + aic * ts, aic * tc - arc * ts], axis=1))
    return mm(jnp.concatenate(tall, axis=0), w2)


def _fft_inv(y_tall, w2i, tc, ts, m1i, ct, n1, mm):
    cc = mm(y_tall, w2i)
    dr, di = [], []
    for c in range(ct):
        cr, ci = cc[c * n1:(c + 1) * n1, :LANES], cc[c * n1:(c + 1) * n1, LANES:]
        dr.append(cr * tc - ci * ts)
        di.append(ci * tc + cr * ts)
    d_wide = jnp.concatenate([jnp.concatenate(dr, axis=1), jnp.concatenate(di, axis=1)], axis=0)
    return mm(m1i, d_wide)


def _hy_filter_fft_kernel(k_ref, m1_ref, tw_ref, w2_ref, kr_ref, ki_ref, *, ct, n1):
    x = _fft_fwd(_wide(k_ref, ct), m1_ref[...], tw_ref[0], tw_ref[1], w2_ref[...], ct, n1, _dot3)
    x = x * (1.0 / (n1 * LANES))
    kr_ref[...] = x[:, :LANES].reshape(ct, n1, LANES)
    ki_ref[...] = x[:, LANES:].reshape(ct, n1, LANES)


def _hy_filter_fft(k4, tabs, ct):
    d, n1, _ = k4.shape
    const = lambda a: pl.BlockSpec(a.shape, lambda i, nd=a.ndim: (0,) * nd)
    blk = pl.BlockSpec((ct, n1, LANES), lambda i: (i, 0, 0))
    return pl.pallas_call(
        functools.partial(_hy_filter_fft_kernel, ct=ct, n1=n1),
        grid=(d // ct,),
        in_specs=[blk, const(tabs["m1_real"]), const(tabs["tw"]), const(tabs["w2_fwd"])],
        out_specs=[blk, blk],
        out_shape=[jax.ShapeDtypeStruct((d, n1, LANES), F32)] * 2,
        compiler_params=_params("parallel"),
        name="hy_filter_fft",
    )(k4, tabs["m1_real"], tabs["tw"], tabs["w2_fwd"])


def _hy_longconv_kernel(z_ref, x0_ref, kr_ref, ki_ref, bias_ref, m1_ref, m1i_ref, tw_ref, w2_ref, w2i_ref,
                        o_ref, *, ct, n1, r):
    tc, ts = tw_ref[0], tw_ref[1]
    x_wide = jnp.concatenate([_wide(z_ref.at[0], ct), _wide(z_ref.at[1], ct)], axis=0)
    x = _fft_fwd(x_wide, m1_ref[...], tc, ts, w2_ref[...], ct, n1, _dot1)
    xr, xi = x[:, :LANES], x[:, LANES:]
    kr = kr_ref[...].reshape(ct * n1, LANES)
    ki = ki_ref[...].reshape(ct * n1, LANES)
    y_tall = jnp.concatenate([xr * kr - xi * ki, xr * ki + xi * kr], axis=1)
    y = _fft_inv(y_tall, w2i_ref[...], tc, ts, m1i_ref[...], ct, n1, _dot1)
    for p in range(2):
        for c in range(ct):
            conv = y[p * r:(p + 1) * r, c * LANES:(c + 1) * LANES]
            o_ref[p, c] = (conv + bias_ref[c] * z_ref[p, c]) * x0_ref[p, c]


def _hy_longconv(z4, x0c4, kr, ki, bias, tabs, ct):
    b, d, r, _ = z4.shape
    n1 = tabs["n1"]
    const = lambda a: pl.BlockSpec(a.shape, lambda i, j, nd=a.ndim: (0,) * nd)
    data = pl.BlockSpec((2, ct, r, LANES), lambda i, j: (j, i, 0, 0))
    spec = pl.BlockSpec((ct, n1, LANES), lambda i, j: (i, 0, 0))
    consts = [tabs["m1_cplx"], tabs["m1_inv"], tabs["tw"], tabs["w2_fwd"], tabs["w2_inv"]]
    return pl.pallas_call(
        functools.partial(_hy_longconv_kernel, ct=ct, n1=n1, r=r),
        grid=(d // ct, b // 2),
        in_specs=[data, data, spec, spec, pl.BlockSpec((ct, 1, 1), lambda i, j: (i, 0, 0))]
                 + [const(a) for a in consts],
        out_specs=data,
        out_shape=jax.ShapeDtypeStruct((b, d, r, LANES), F32),
        compiler_params=_params("parallel", "parallel"),
        name="hy_longconv",
    )(z4, x0c4, kr, ki, bias.reshape(d, 1, 1), *consts)


def _router_tail(x1, m, gf_ref, wrt_ref, x1_ref, hf_ref, aff_ref):
    x1_ref[0] = x1
    hf = _norm_mod(x1, gf_ref[...], m[3:4], m[4:5]).astype(BF16)
    hf_ref[0] = hf
    logits = _dot_nt(wrt_ref[...], hf)
    ex = jnp.exp(logits - jnp.max(logits, axis=0, keepdims=True))
    aff_ref[0] = ex / jnp.sum(ex, axis=0, keepdims=True)


def _hy_outproj_kernel(y_ref, x_ref, m_ref, w_ref, b_ref, gf_ref, wrt_ref, x1_ref, hf_ref, aff_ref):
    m = m_ref[0]
    y = jnp.transpose(y_ref[0]).astype(BF16)
    x1 = x_ref[0] + m[2:3] * (_dot(y, w_ref[...]) + b_ref[...])
    _router_tail(x1, m, gf_ref, wrt_ref, x1_ref, hf_ref, aff_ref)


def _mla_outproj_kernel(y_ref, x_ref, m_ref, w_ref, gf_ref, wrt_ref, x1_ref, hf_ref, aff_ref):
    m = m_ref[0]
    x1 = x_ref[0] + m[2:3] * _dot(y_ref[0].astype(BF16), w_ref[...])
    _router_tail(x1, m, gf_ref, wrt_ref, x1_ref, hf_ref, aff_ref)


def _outproj(y, x, mod, w_out, b_out, g_ffn, w_router_t, tt, channel_major):
    b, s, d = x.shape
    e = w_router_t.shape[0]
    dy = w_out.shape[0]
    tile = pl.BlockSpec((1, tt, d), lambda i, j: (i, j, 0))
    const = lambda shape: pl.BlockSpec(shape, lambda i, j: (0, 0))
    if channel_major:
        y_spec = pl.BlockSpec((1, dy, tt), lambda i, j: (i, 0, j))
        kern, extra, extra_specs = _hy_outproj_kernel, [b_out.reshape(1, d)], [const((1, d))]
    else:
        y_spec = pl.BlockSpec((1, tt, dy), lambda i, j: (i, j, 0))
        kern, extra, extra_specs = _mla_outproj_kernel, [], []
    return pl.pallas_call(
        kern,
        grid=(b, s // tt),
        in_specs=[y_spec, tile, pl.BlockSpec((1, 6, d), lambda i, j: (i, 0, 0)), const((dy, d))]
                 + extra_specs + [const((1, d)), const((e, d))],
        out_specs=[tile, tile, pl.BlockSpec((1, e, tt), lambda i, j: (i, 0, j))],
        out_shape=[jax.ShapeDtypeStruct((b, s, d), F32), jax.ShapeDtypeStruct((b, s, d), BF16),
                   jax.ShapeDtypeStruct((b, e, s), F32)],
        compiler_params=_params("parallel", "parallel"),
        name="outproj_router",
    )(y, x, mod, w_out, *extra, g_ffn.reshape(1, d), w_router_t)


def _route_kernel(a_ref, o_ref, idx_ref, cs_ref, *, n_exp, n_chunk, cap):
    rows = n_exp * n_chunk
    a = a_ref[0].reshape(rows, LANES)
    ri = lax.broadcasted_iota(I32, (rows, rows), 0)
    rj = lax.broadcasted_iota(I32, (rows, rows), 1)
    same = (ri // n_chunk) == (rj // n_chunk)
    blk_all = jnp.where(same, 1.0, 0.0).astype(BF16)
    blk_before = jnp.where(same & (rj < ri), 1.0, 0.0).astype(BF16)
    li = lax.broadcasted_iota(I32, (LANES, LANES), 0)
    lj = lax.broadcasted_iota(I32, (LANES, LANES), 1)
    ones = jnp.ones((LANES, LANES), BF16)
    before = jnp.where(li < lj, 1.0, 0.0).astype(BF16)

    def as_bf16(mask):
        return jnp.where(mask, 1.0, 0.0).astype(BF16)

    def expert_count(mask):
        return _dot(blk_all, _dot(as_bf16(mask), ones).astype(BF16))

    def prefix(mask):
        mb = as_bf16(mask)
        return _dot(mb, before) + _dot(blk_before, _dot(mb, ones).astype(BF16))

    cur = jnp.zeros((rows, LANES), I32)
    for bit in range(30, -1, -1):
        cand = cur | (1 << bit)
        cur = jnp.where(expert_count(a >= pltpu.bitcast(cand, F32)) >= cap, cand, cur)
    lo = pltpu.bitcast(cur, F32)
    hi = pltpu.bitcast(cur + 1, F32)
    for _ in range(16):
        mid = lo + (hi - lo) * 0.5
        take = expert_count(a >= mid) >= cap
        lo = jnp.where(take, mid, lo)
        hi = jnp.where(take, hi, mid)
    gt = a >= hi
    eq = (a >= lo) & (a < hi)
    need = cap - expert_count(gt)
    sel = gt | (eq & (prefix(eq) < need))
    sel_b = as_bf16(sel)
    chunk_total = _dot(sel_b, ones)
    chunk_start = _dot(blk_before, chunk_total.astype(BF16))
    within = _dot(sel_b, before)
    o_ref[0] = jnp.where(sel, (chunk_start + within).astype(I32), -1).reshape(n_exp, n_chunk, LANES)
    cs_ref[0] = chunk_start[:, 0:1].astype(I32).reshape(n_exp, n_chunk, 1)

    rank = jnp.where(sel, within + 1.0, 0.0).astype(BF16)
    start_row = jnp.transpose(chunk_start)[0:1, :]
    end_row = jnp.transpose(chunk_start + chunk_total)[0:1, :]
    s_col = lax.broadcasted_iota(I32, (cap, rows), 0).astype(F32)
    row_id = lax.broadcasted_iota(I32, (cap, rows), 1)
    lane_f = lax.broadcasted_iota(I32, (cap, LANES), 1).astype(F32)
    for ex in range(n_exp):
        in_chunk = ((row_id // n_chunk) == ex) & (start_row <= s_col) & (s_col < end_row)
        rank_rows = _dot(jnp.where(in_chunk, 1.0, 0.0).astype(BF16), rank)
        start_s = jnp.sum(jnp.where(in_chunk, start_row, 0.0), axis=1, keepdims=True)
        chunk_s = jnp.sum(jnp.where(in_chunk, (row_id - ex * n_chunk).astype(F32), 0.0), axis=1, keepdims=True)
        match = rank_rows == (s_col[:, 0:1] + 1.0 - start_s)
        lane_s = jnp.sum(jnp.where(match, lane_f, 0.0), axis=1, keepdims=True)
        idx_ref[0, ex] = (chunk_s * LANES + lane_s).astype(I32)


def _route(aff_t, cap):
    b, e, t = aff_t.shape
    nc = t // LANES
    blk = pl.BlockSpec((1, e, nc, LANES), lambda i: (i, 0, 0, 0))
    slot, idx, chunk_start = pl.pallas_call(
        functools.partial(_route_kernel, n_exp=e, n_chunk=nc, cap=cap),
        grid=(b,),
        in_specs=[blk],
        out_specs=[blk, pl.BlockSpec((1, e, cap, 1), lambda i: (i, 0, 0, 0)),
                   pl.BlockSpec((1, e, nc, 1), lambda i: (i, 0, 0, 0))],
        out_shape=[jax.ShapeDtypeStruct((b, e, nc, LANES), I32), jax.ShapeDtypeStruct((b, e, cap, 1), I32),
                   jax.ShapeDtypeStruct((b, e, nc, 1), I32)],
        compiler_params=_params("parallel"),
        name="route_topc",
    )(aff_t.reshape(b, e, nc, LANES))
    return slot.reshape(b, e, t), idx.reshape(b, e, cap), chunk_start.reshape(b, e, nc)


SCATTER_UNROLL = 8


def _expert_kernel(idx_ref, cs_ref, hf_ref, slot_ref, aff_ref, wg_ref, wu_ref, wd_ref, acc_ref, ye_ref, xs_ref,
                   gsl_ref, *, cap, tk, win):
    t, d = hf_ref.shape[1], hf_ref.shape[2]
    nc = d // LANES

    @pl.when(pl.program_id(1) == 0)
    def _():
        acc_ref[...] = jnp.zeros_like(acc_ref)

    xs_ref[...] = jnp.zeros_like(xs_ref)
    gsl_ref[...] = jnp.zeros_like(gsl_ref)
    n_tk = t // tk
    for k in range(n_tk):
        cols = slice(k * tk, (k + 1) * tk)

        def gather(n_rows, first):
            hit = (lax.broadcasted_iota(I32, (n_rows, tk), 0) + first) == slot_ref[0, 0, :, cols]
            x = _dot(jnp.where(hit, 1.0, 0.0).astype(BF16), hf_ref[0, cols, :])
            g = jnp.sum(jnp.where(hit, aff_ref[0, 0, :, cols], 0.0), axis=1, keepdims=True)
            return x, g

        lo = cs_ref[0, 0, 0, k * (tk // LANES)]
        hi = cs_ref[0, 0, 0, (k + 1) * (tk // LANES)] if k + 1 < n_tk else cap
        first = jnp.minimum((lo // 8) * 8, cap - win)
        fits = (hi - first) <= win

        @pl.when(fits)
        def _():
            x, g = gather(win, first)
            rows = pl.ds(pl.multiple_of(first, 8), win)
            xs_ref[rows, :] += x
            gsl_ref[rows, :] += g

        @pl.when(jnp.logical_not(fits))
        def _():
            x, g = gather(cap, 0)
            xs_ref[...] += x
            gsl_ref[...] += g

    xs = xs_ref[...]
    gsl = gsl_ref[...]
    xs = xs.astype(BF16)
    gate = _dot(xs, wg_ref[0])
    up = _dot(xs, wu_ref[0])
    hid = gate / (1.0 + jnp.exp(-gate)) * up
    ye = _dot(hid.astype(BF16), wd_ref[0]) * gsl
    for c in range(nc):
        ye_ref[pl.ds(c, cap, stride=nc), :] = ye[:, c * LANES:(c + 1) * LANES]

    def tile_of(i):
        return pl.ds(pl.multiple_of(i * nc, nc), nc)

    def scatter_group(g, carry):
        base = g * SCATTER_UNROLL
        toks = [idx_ref[0, 0, 0, base + u] for u in range(SCATTER_UNROLL)]
        sums = [acc_ref[0, tile_of(toks[u]), :] + ye_ref[tile_of(base + u), :] for u in range(SCATTER_UNROLL)]
        for u in range(SCATTER_UNROLL):
            acc_ref[0, tile_of(toks[u]), :] = sums[u]
        return carry

    lax.fori_loop(0, cap // SCATTER_UNROLL, scatter_group, 0)


def _experts(hf, idx, chunk_start, slot_row, aff_row, wg, wu, wd, cap):
    b, t, d = hf.shape
    e, _, f = wg.shape
    nc = d // LANES
    tk = max(LANES, t // 4)
    win = max(8, cap // 2)
    row = pl.BlockSpec((1, 1, 1, t), lambda i, j: (i, j, 0, 0))
    smem_row = lambda n: pl.BlockSpec((1, 1, 1, n), lambda i, j: (i, j, 0, 0), memory_space=pltpu.SMEM)
    once = pl.Buffered(1)
    return pl.pallas_call(
        functools.partial(_expert_kernel, cap=cap, tk=tk, win=win),
        grid=(b, e),
        in_specs=[smem_row(cap), smem_row(t // LANES),
                  pl.BlockSpec((1, t, d), lambda i, j: (i, 0, 0), pipeline_mode=once), row, row,
                  pl.BlockSpec((1, d, f), lambda i, j: (j, 0, 0)),
                  pl.BlockSpec((1, d, f), lambda i, j: (j, 0, 0)),
                  pl.BlockSpec((1, f, d), lambda i, j: (j, 0, 0))],
        out_specs=pl.BlockSpec((1, t * nc, LANES), lambda i, j: (i, 0, 0), pipeline_mode=once),
        out_shape=jax.ShapeDtypeStruct((b, t * nc, LANES), F32),
        scratch_shapes=[pltpu.VMEM((cap * nc, LANES), F32), pltpu.VMEM((cap, d), F32), pltpu.VMEM((cap, 1), F32)],
        compiler_params=_params("parallel", "arbitrary"),
        name="expert_ffn",
    )(idx.reshape(b, e, 1, cap), chunk_start.reshape(b, e, 1, t // LANES), hf, slot_row, aff_row, wg, wu, wd)


def _combine_kernel(acc_ref, x_ref, m_ref, gn_ref, o_ref, *, final_norm):
    tt, d = o_ref.shape[1], o_ref.shape[2]
    nc = d // LANES
    moe = jnp.concatenate([acc_ref[0, pl.ds(c, tt, stride=nc), :] for c in range(nc)], axis=1)
    x2 = x_ref[0] + m_ref[0][5:6] * moe
    o_ref[0] = _rms(x2, gn_ref[...]) if final_norm else x2


def _combine(acc, x1, mod, g_final, tt, final_norm):
    b, t, d = x1.shape
    nc = d // LANES
    tile = pl.BlockSpec((1, tt, d), lambda i, j: (i, j, 0))
    return pl.pallas_call(
        functools.partial(_combine_kernel, final_norm=final_norm),
        grid=(b, t // tt),
        in_specs=[pl.BlockSpec((1, tt * nc, LANES), lambda i, j: (i, j, 0)),
                  tile, pl.BlockSpec((1, 6, d), lambda i, j: (i, 0, 0)),
                  pl.BlockSpec((1, d), lambda i, j: (0, 0))],
        out_specs=tile,
        out_shape=jax.ShapeDtypeStruct((b, t, d), F32),
        compiler_params=_params("parallel", "parallel"),
        name="moe_combine",
    )(acc, x1, mod, g_final.reshape(1, d))


def _moe(hf, aff_t, x1, mod, wg, wu, wd, g_final, final_norm):
    b, t, d = hf.shape
    e = aff_t.shape[1]
    cap = EC_CAPACITY_FACTOR * t // e
    slot, idx, chunk_start = _route(aff_t, cap)
    acc = _experts(hf, idx, chunk_start, slot.reshape(b, e, 1, t), aff_t.reshape(b, e, 1, t), wg, wu, wd, cap)
    return _combine(acc, x1, mod, g_final, min(512, t), final_norm)


def _rope(x, cos_t, sin_lo, sin_hi):
    return x * cos_t + pltpu.roll(x, LANES - QK_ROPE // 2, 1) * sin_lo + pltpu.roll(x, QK_ROPE // 2, 1) * sin_hi


def _mla_proj_kernel(x_ref, m_ref, g_ref, pos_ref, invf_ref, win_ref, qg_ref, wq_ref, kg_ref, wkn_ref, wv_ref,
                     q_ref, k_ref, v_ref):
    m = m_ref[0]
    h = _norm_mod(x_ref[0], g_ref[...], m[0:1], m[1:2]).astype(BF16)
    a = _dot(h, win_ref[...])
    cq = a[:, :Q_LORA]
    ckv = a[:, Q_LORA:Q_LORA + KV_LORA]
    kpe = a[:, Q_LORA + KV_LORA:]
    q = _dot(_rms(cq, qg_ref[...]).astype(BF16), wq_ref[...])
    kvn = _rms(ckv, kg_ref[...]).astype(BF16)
    kn = _dot(kvn, wkn_ref[...])
    v_ref[0] = _dot(kvn, wv_ref[...]).astype(BF16)
    ang = pos_ref[0] * invf_ref[...]
    lane = lax.broadcasted_iota(I32, ang.shape, 1)
    cos_t = jnp.where(lane < QK_ROPE, jnp.cos(ang), 0.0)
    sin_a = jnp.sin(ang)
    sin_lo = jnp.where(lane < QK_ROPE // 2, -sin_a, 0.0)
    sin_hi = jnp.where((lane >= QK_ROPE // 2) & (lane < QK_ROPE), sin_a, 0.0)
    kpe = _rope(kpe, cos_t, sin_lo, sin_hi)
    qs, ks = [], []
    for hd in range(MLA_HEADS):
        qs.append(q[:, hd * HEAD_PAD:hd * HEAD_PAD + QK_NOPE])
        qs.append(_rope(q[:, hd * HEAD_PAD + QK_NOPE:(hd + 1) * HEAD_PAD], cos_t, sin_lo, sin_hi))
        ks.append(kn[:, hd * QK_NOPE:(hd + 1) * QK_NOPE])
        ks.append(kpe)
    q_ref[0] = jnp.concatenate(qs, axis=1).astype(BF16)
    k_ref[0] = jnp.concatenate(ks, axis=1).astype(BF16)


def _mla_proj(x, mod, g, positions, w_in, q_g, w_qb, kv_g, w_kvb, tt):
    b, s, d = x.shape
    nh = MLA_HEADS
    pad = HEAD_PAD - QK_NOPE - QK_ROPE
    win = jnp.concatenate([w_in, jnp.zeros((d, LANES - QK_ROPE), F32)], axis=1).astype(BF16)
    wq = w_qb.reshape(Q_LORA, nh, QK_NOPE + QK_ROPE)
    wq = jnp.concatenate([wq, jnp.zeros((Q_LORA, nh, pad), F32)], axis=2).reshape(Q_LORA, nh * HEAD_PAD).astype(BF16)
    wkv = w_kvb.reshape(KV_LORA, nh, QK_NOPE + V_HEAD)
    wkn = wkv[:, :, :QK_NOPE].reshape(KV_LORA, nh * QK_NOPE).astype(BF16)
    wv = wkv[:, :, QK_NOPE:].reshape(KV_LORA, nh * V_HEAD).astype(BF16)
    inv_freq = ROPE_THETA ** (-jnp.arange(0, QK_ROPE, 2, dtype=F32) / QK_ROPE)
    invf = jnp.concatenate([inv_freq, inv_freq, jnp.zeros((LANES - QK_ROPE,), F32)]).reshape(1, LANES)
    pos = positions.astype(F32).reshape(b, s, 1)
    const = lambda a: pl.BlockSpec(a.shape, lambda i, j: (0, 0))
    tile = lambda w: pl.BlockSpec((1, tt, w), lambda i, j: (i, j, 0))
    g2, qg2, kg2 = g.reshape(1, d), q_g.reshape(1, Q_LORA), kv_g.reshape(1, KV_LORA)
    return pl.pallas_call(
        _mla_proj_kernel,
        grid=(b, s // tt),
        in_specs=[tile(d), pl.BlockSpec((1, 6, d), lambda i, j: (i, 0, 0)), const(g2), tile(1), const(invf),
                  const(win), const(qg2), const(wq), const(kg2), const(wkn), const(wv)],
        out_specs=[tile(nh * HEAD_PAD), tile(nh * HEAD_PAD), tile(nh * V_HEAD)],
        out_shape=[jax.ShapeDtypeStruct((b, s, nh * HEAD_PAD), BF16),
                   jax.ShapeDtypeStruct((b, s, nh * HEAD_PAD), BF16),
                   jax.ShapeDtypeStruct((b, s, nh * V_HEAD), BF16)],
        compiler_params=_params("parallel", "parallel"),
        name="mla_proj",
    )(x, mod, g2, pos, invf, win, qg2, wq, kg2, wkn, wv)


def _mla_attn_kernel(q_ref, k_ref, v_ref, o_ref, *, n_sub):
    sub = q_ref.shape[1] // n_sub
    for i in range(n_sub):
        s = _dot_nt(q_ref[0, i * sub:(i + 1) * sub, :], k_ref[0])
        p = jnp.exp2((s - jnp.max(s, axis=1, keepdims=True)) * (SOFTMAX_SCALE * math.log2(math.e)))
        o = _dot(p.astype(BF16), v_ref[0])
        o_ref[0, i * sub:(i + 1) * sub, :] = o / jnp.sum(p, axis=1, keepdims=True)


def _mla_attn(q, k, v, tq):
    b, s, _ = q.shape
    nh = MLA_HEADS
    return pl.pallas_call(
        functools.partial(_mla_attn_kernel, n_sub=max(1, tq // 256)),
        grid=(b, nh, s // tq),
        in_specs=[pl.BlockSpec((1, tq, HEAD_PAD), lambda i, h, j: (i, j, h)),
                  pl.BlockSpec((1, s, HEAD_PAD), lambda i, h, j: (i, 0, h)),
                  pl.BlockSpec((1, s, V_HEAD), lambda i, h, j: (i, 0, h))],
        out_specs=pl.BlockSpec((1, tq, V_HEAD), lambda i, h, j: (i, j, h)),
        out_shape=jax.ShapeDtypeStruct((b, s, nh * V_HEAD), F32),
        compiler_params=_params("parallel", "parallel", "parallel"),
        name="mla_attn",
    )(q, k, v)


def kernel(x, c, positions, ada_w, ada_b, norm_mix_g, norm_ffn_g, hy_w_in, hy_b_in, hy_conv_w, hy_conv_b, hy_f_w1, hy_f_b1, hy_f_w2, hy_f_b2, hy_f_w3, hy_f_freq, hy_f_bias, hy_w_out, hy_b_out, mla_w_in, mla_q_norm_g, mla_w_qb, mla_kv_norm_g, mla_w_kvb, mla_w_out, moe_w_router, moe_w_gate, moe_w_up, moe_w_down, final_norm_g):
    b, s, d = x.shape
    depth = ada_w.shape[0]
    assert s % LANES == 0 and b % 2 == 0 and depth == 2
    tt = min(512, s)
    mod = _ada(c, ada_w, ada_b).reshape(depth, b, 6, d)

    u_t = _hy_inproj(x, mod[0], norm_mix_g[0], hy_w_in[0].T.astype(BF16), hy_b_in[0], tt)
    cwb = jnp.concatenate([hy_conv_w[0].T, hy_conv_b[0][:, None]], axis=1)
    z_t, x0c_t = _hy_conv_gate(u_t, cwb, d, min(64, d))
    tabs = _dft_tables(s)
    n1, r = tabs["n1"], tabs["r"]
    k_t = _hy_filter(hy_f_w1[0], hy_f_b1[0], hy_f_w2[0], hy_f_b2[0], hy_f_w3[0], hy_f_freq[0], d, s)
    ct = min(16, d)
    kr, ki = _hy_filter_fft(k_t.reshape(d, n1, LANES), tabs, ct)
    yg = _hy_longconv(z_t.reshape(b, d, r, LANES), x0c_t.reshape(b, d, r, LANES), kr, ki,
                      hy_f_bias[0, 0], tabs, ct).reshape(b, d, s)
    x1, hf, aff_t = _outproj(yg, x, mod[0], hy_w_out[0].astype(BF16), hy_b_out[0], norm_ffn_g[0],
                             moe_w_router[0].T.astype(BF16), tt, True)
    x2 = _moe(hf, aff_t, x1, mod[0], moe_w_gate[0].astype(BF16), moe_w_up[0].astype(BF16),
              moe_w_down[0].astype(BF16), final_norm_g, False)

    q, k, v = _mla_proj(x2, mod[1], norm_mix_g[1], positions, mla_w_in[0], mla_q_norm_g[0], mla_w_qb[0],
                        mla_kv_norm_g[0], mla_w_kvb[0], tt)
    o = _mla_attn(q, k, v, min(1024, s))
    x3, hf, aff_t = _outproj(o, x2, mod[1], mla_w_out[0].astype(BF16), None, norm_ffn_g[1],
                             moe_w_router[1].T.astype(BF16), tt, False)
    return _moe(hf, aff_t, x3, mod[1], moe_w_gate[1].astype(BF16), moe_w_up[1].astype(BF16),
                moe_w_down[1].astype(BF16), final_norm_g, True)
```

```python
import functools
import math

import numpy as np
import jax
import jax.numpy as jnp
from jax import lax
from jax.experimental import pallas as pl
from jax.experimental.pallas import tpu as pltpu

F32 = jnp.float32
BF16 = jnp.bfloat16
I32 = jnp.int32

EPS = 1e-6
LANES = 128
VMEM_LIMIT = 56 * 1024 * 1024

HY_BANDS = 16
HY_FAST_DECAY_PCT = 0.3
HY_SLOW_DECAY_PCT = 1.5
HY_DECAY_TARGET = 1e-2

MLA_HEADS = 8
QK_NOPE = 128
QK_ROPE = 64
V_HEAD = 128
Q_LORA = 256
KV_LORA = 128
ROPE_THETA = 10000.0
SOFTMAX_SCALE = 1.0 / math.sqrt(QK_NOPE + QK_ROPE)
HEAD_PAD = 256

EC_CAPACITY_FACTOR = 2

NT_DIMS = (((1,), (1,)), ((), ()))


def _params(*sem):
    return pltpu.CompilerParams(dimension_semantics=sem, vmem_limit_bytes=VMEM_LIMIT)


def _dot(a, b):
    return jnp.dot(a, b, preferred_element_type=F32)


def _dot_nt(a, b):
    return lax.dot_general(a, b, NT_DIMS, preferred_element_type=F32)


def _split(a):
    hi = a.astype(BF16)
    lo = (a - hi.astype(F32)).astype(BF16)
    return hi, lo


def _dot3(a, b):
    ah, al = _split(a)
    bh, bl = _split(b)
    return _dot(ah, bh) + (_dot(ah, bl) + _dot(al, bh))


def _dot1(a, b):
    return _dot(a.astype(BF16), b.astype(BF16))


def _rms(x, g):
    return x * lax.rsqrt(jnp.mean(x * x, axis=-1, keepdims=True) + EPS) * g


def _norm_mod(x, g, shift, scale):
    return _rms(x, g) * (1.0 + scale) + shift


def _ada_kernel(c_ref, w_ref, b_ref, o_ref):
    c = c_ref[...]
    cs = c / (1.0 + jnp.exp(-c))
    o_ref[0] = _dot(cs.astype(BF16), w_ref[0].astype(BF16)) + b_ref[0]


def _ada(c, ada_w, ada_b):
    depth, d, n = ada_w.shape
    b = c.shape[0]
    tn = n // 4
    return pl.pallas_call(
        _ada_kernel,
        grid=(depth, n // tn),
        in_specs=[pl.BlockSpec((b, d), lambda i, j: (0, 0)),
                  pl.BlockSpec((1, d, tn), lambda i, j: (i, 0, j)),
                  pl.BlockSpec((1, 1, tn), lambda i, j: (i, 0, j))],
        out_specs=pl.BlockSpec((1, b, tn), lambda i, j: (i, 0, j)),
        out_shape=jax.ShapeDtypeStruct((depth, b, n), F32),
        compiler_params=_params("parallel", "parallel"),
        name="ada_mod",
    )(c, ada_w, ada_b.reshape(depth, 1, n))


def _hy_inproj_kernel(x_ref, m_ref, g_ref, wt_ref, b_ref, o_ref):
    m = m_ref[0]
    h = _norm_mod(x_ref[0], g_ref[...], m[0:1], m[1:2])
    o_ref[0] = _dot_nt(wt_ref[...], h.astype(BF16)) + b_ref[...]


def _hy_inproj(x, mod, g, w_in_t, b_in, tt):
    b, s, d = x.shape
    n = w_in_t.shape[0]
    return pl.pallas_call(
        _hy_inproj_kernel,
        grid=(b, s // tt),
        in_specs=[pl.BlockSpec((1, tt, d), lambda i, j: (i, j, 0)),
                  pl.BlockSpec((1, 6, d), lambda i, j: (i, 0, 0)),
                  pl.BlockSpec((1, d), lambda i, j: (0, 0)),
                  pl.BlockSpec((n, d), lambda i, j: (0, 0)),
                  pl.BlockSpec((n, 1), lambda i, j: (0, 0))],
        out_specs=pl.BlockSpec((1, n, tt), lambda i, j: (i, 0, j)),
        out_shape=jax.ShapeDtypeStruct((b, n, s), F32),
        compiler_params=_params("parallel", "parallel"),
        name="hy_inproj",
    )(x, mod, g.reshape(1, d), w_in_t, b_in.reshape(n, 1))


def _short_conv(u, w):
    length = u.shape[1]
    lane = lax.broadcasted_iota(I32, (u.shape[0], LANES), 1)
    prev = pltpu.roll(u, 1, 1)
    nxt = pltpu.roll(u, length - 1, 1)
    prev = jnp.concatenate([jnp.where(lane == 0, 0.0, prev[:, :LANES]), prev[:, LANES:]], axis=1)
    nxt = jnp.concatenate([nxt[:, :length - LANES], jnp.where(lane == LANES - 1, 0.0, nxt[:, length - LANES:])], axis=1)
    return w[:, 0:1] * prev + w[:, 1:2] * u + w[:, 2:3] * nxt + w[:, 3:4]


def _hy_conv_gate_kernel(x0_ref, x1_ref, v_ref, w0_ref, w1_ref, wv_ref, z_ref, x0c_ref):
    ct, s = v_ref.shape[1], v_ref.shape[2]
    r = s // LANES
    z = _short_conv(v_ref[0], wv_ref[...]) * _short_conv(x1_ref[0], w1_ref[...])
    x0c = _short_conv(x0_ref[0], w0_ref[...])
    for j in range(r):
        rows = pl.ds(j, ct, stride=r)
        z_ref[0, rows, :] = z[:, j * LANES:(j + 1) * LANES]
        x0c_ref[0, rows, :] = x0c[:, j * LANES:(j + 1) * LANES]


def _hy_conv_gate(u_t, cwb, d, ct):
    b, _, s = u_t.shape
    nb = d // ct
    r = s // LANES
    u_spec = lambda p: pl.BlockSpec((1, ct, s), lambda i, j, p=p: (i, p * nb + j, 0))
    w_spec = lambda p: pl.BlockSpec((ct, 4), lambda i, j, p=p: (p * nb + j, 0))
    out_spec = pl.BlockSpec((1, ct * r, LANES), lambda i, j: (i, j, 0))
    z, x0c = pl.pallas_call(
        _hy_conv_gate_kernel,
        grid=(b, nb),
        in_specs=[u_spec(0), u_spec(1), u_spec(2), w_spec(0), w_spec(1), w_spec(2)],
        out_specs=[out_spec, out_spec],
        out_shape=[jax.ShapeDtypeStruct((b, d * r, LANES), F32)] * 2,
        compiler_params=_params("parallel", "parallel"),
        name="hy_conv_gate",
    )(u_t, u_t, u_t, cwb, cwb, cwb)
    return z.reshape(b, d, r, LANES), x0c.reshape(b, d, r, LANES)


def _hy_filter_kernel(w1t_ref, b1_ref, w2t_ref, b2_ref, w3t_ref, fr_ref, o_ref, *, length, lt, nfeat):
    half = pl.program_id(0)
    j = pl.program_id(1)
    d = o_ref.shape[0]
    rows = w1t_ref.shape[1]
    hi = lax.Precision.HIGHEST
    t = half * length + j * lt + lax.broadcasted_iota(I32, (rows, lt), 1)
    pos = jnp.where(half == 0, t, 2 * length - t).astype(F32)
    r = lax.broadcasted_iota(I32, (rows, lt), 0)
    tlin = pos / (length - 1.0)
    w = 2.0 * math.pi * pos / length
    fidx = jnp.where(r > HY_BANDS, r - (HY_BANDS + 1), r - 1).astype(F32)
    f = 1e-4 + fidx * ((HY_BANDS - 1 - 1e-4) / (HY_BANDS - 1))
    ang = f * w
    feat = jnp.where(r == 0, tlin, jnp.where(r <= HY_BANDS, jnp.cos(ang), -jnp.sin(ang)))
    feat = jnp.where(r < nfeat, feat, 0.0)
    fr = fr_ref[...]
    h = jnp.sin(fr * (jnp.dot(w1t_ref[...], feat, precision=hi, preferred_element_type=F32) + b1_ref[...]))
    h = jnp.sin(fr * (jnp.dot(w2t_ref[...], h, precision=hi, preferred_element_type=F32) + b2_ref[...]))
    out = jnp.dot(w3t_ref[...], h, precision=hi, preferred_element_type=F32)
    c = lax.broadcasted_iota(I32, (d, lt), 0).astype(F32)
    min_decay = math.log(HY_DECAY_TARGET) / HY_FAST_DECAY_PCT
    max_decay = math.log(HY_DECAY_TARGET) / HY_SLOW_DECAY_PCT
    delta = min_decay + c * ((max_decay - min_decay) / (d - 1.0))
    decay = jnp.exp(-tlin[0:1, :] * jnp.abs(delta))
    valid = pos[0:1, :] < length
    o_ref[...] = jnp.where(valid, out * decay, 0.0)


def _hy_filter(f_w1, f_b1, f_w2, f_b2, f_w3, f_freq, d, length):
    nfeat, hid = f_w1.shape
    rows = ((nfeat + 7) // 8) * 8
    w1t = jnp.zeros((hid, rows), F32).at[:, :nfeat].set(f_w1.T)
    lt = min(512, length)
    col = lambda a: a.reshape(hid, 1)
    small = lambda shape: pl.BlockSpec(shape, lambda h, j: (0, 0))
    return pl.pallas_call(
        functools.partial(_hy_filter_kernel, length=length, lt=lt, nfeat=nfeat),
        grid=(2, length // lt),
        in_specs=[small((hid, rows)), small((hid, 1)), small((hid, hid)), small((hid, 1)),
                  pl.BlockSpec((d, hid), lambda h, j: (h, 0)), small((hid, 1))],
        out_specs=pl.BlockSpec((d, lt), lambda h, j: (0, h * (length // lt) + j)),
        out_shape=jax.ShapeDtypeStruct((d, 2 * length), F32),
        compiler_params=_params("parallel", "parallel"),
        name="hy_filter",
    )(w1t, col(f_b1), f_w2.T, col(f_b2), f_w3.T, col(f_freq))


def _dft_tables(length):
    n = 2 * length
    n1 = n // LANES
    r = n1 // 2
    k1 = np.arange(n1)[:, None]
    ang1 = 2.0 * np.pi * k1 * np.arange(n1)[None, :] / n1
    c1, s1 = np.cos(ang1), np.sin(ang1)
    m1_real = np.concatenate([c1, -s1], axis=0)
    m1_cplx = np.block([[c1[:, :r], s1[:, :r]], [-s1[:, :r], c1[:, :r]]])
    m1_inv = np.block([[c1[:r, :], -s1[:r, :]], [s1[:r, :], c1[:r, :]]])
    ang_t = 2.0 * np.pi * k1 * np.arange(LANES)[None, :] / n
    tw = np.stack([np.cos(ang_t), np.sin(ang_t)])
    n2 = np.arange(LANES)
    ang2 = 2.0 * np.pi * n2[:, None] * n2[None, :] / LANES
    c2, s2 = np.cos(ang2), np.sin(ang2)
    w2_fwd = np.block([[c2, -s2], [s2, c2]])
    w2_inv = np.block([[c2, s2], [-s2, c2]])
    f = lambda a: jnp.asarray(a, F32)
    return dict(n1=n1, r=r, m1_real=f(m1_real), m1_cplx=f(m1_cplx), m1_inv=f(m1_inv),
                tw=f(tw), w2_fwd=f(w2_fwd), w2_inv=f(w2_inv))


def _wide(ref3, ct):
    return jnp.concatenate([ref3[c] for c in range(ct)], axis=1)


def _fft_fwd(x_wide, m1, tc, ts, w2, ct, n1, mm):
    a = mm(m1, x_wide)
    ar, ai = a[:n1], a[n1:]
    tall = []
    for c in range(ct):
        arc, aic = ar[:, c * LANES:(c + 1) * LANES], ai[:, c * LANES:(c + 1) * LANES]
        tall.append(jnp.concatenate([arc * tc + aic * ts, aic * tc - arc * ts], axis=1))
    return mm(jnp.concatenate(tall, axis=0), w2)


def _fft_inv(y_tall, w2i, tc, ts, m1i, ct, n1, mm):
    cc = mm(y_tall, w2i)
    dr, di = [], []
    for c in range(ct):
        cr, ci = cc[c * n1:(c + 1) * n1, :LANES], cc[c * n1:(c + 1) * n1, LANES:]
        dr.append(cr * tc - ci * ts)
        di.append(ci * tc + cr * ts)
    d_wide = jnp.concatenate([jnp.concatenate(dr, axis=1), jnp.concatenate(di, axis=1)], axis=0)
    return mm(m1i, d_wide)


def _hy_filter_fft_kernel(k_ref, m1_ref, tw_ref, w2_ref, kr_ref, ki_ref, *, ct, n1):
    x = _fft_fwd(_wide(k_ref, ct), m1_ref[...], tw_ref[0], tw_ref[1], w2_ref[...], ct, n1, _dot3)
    x = x * (1.0 / (n1 * LANES))
    kr_ref[...] = x[:, :LANES].reshape(ct, n1, LANES)
    ki_ref[...] = x[:, LANES:].reshape(ct, n1, LANES)


def _hy_filter_fft(k4, tabs, ct):
    d, n1, _ = k4.shape
    const = lambda a: pl.BlockSpec(a.shape, lambda i, nd=a.ndim: (0,) * nd)
    blk = pl.BlockSpec((ct, n1, LANES), lambda i: (i, 0, 0))
    return pl.pallas_call(
        functools.partial(_hy_filter_fft_kernel, ct=ct, n1=n1),
        grid=(d // ct,),
        in_specs=[blk, const(tabs["m1_real"]), const(tabs["tw"]), const(tabs["w2_fwd"])],
        out_specs=[blk, blk],
        out_shape=[jax.ShapeDtypeStruct((d, n1, LANES), F32)] * 2,
        compiler_params=_params("parallel"),
        name="hy_filter_fft",
    )(k4, tabs["m1_real"], tabs["tw"], tabs["w2_fwd"])


def _hy_longconv_kernel(z_ref, x0_ref, kr_ref, ki_ref, bias_ref, m1_ref, m1i_ref, tw_ref, w2_ref, w2i_ref,
                        o_ref, y_ref, *, ct, n1, r):
    tc, ts = tw_ref[0], tw_ref[1]
    x_wide = jnp.concatenate([_wide(z_ref.at[0], ct), _wide(z_ref.at[1], ct)], axis=0)
    x = _fft_fwd(x_wide, m1_ref[...], tc, ts, w2_ref[...], ct, n1, _dot1)
    xr, xi = x[:, :LANES], x[:, LANES:]
    kr = kr_ref[...].reshape(ct * n1, LANES)
    ki = ki_ref[...].reshape(ct * n1, LANES)
    y_tall = jnp.concatenate([xr * kr - xi * ki, xr * ki + xi * kr], axis=1)
    y = _fft_inv(y_tall, w2i_ref[...], tc, ts, m1i_ref[...], ct, n1, _dot1)
    for p in range(2):
        for c in range(ct):
            conv = y[p * r:(p + 1) * r, c * LANES:(c + 1) * LANES]
            y_ref[p, c * r:(c + 1) * r, :] = (conv + bias_ref[c] * z_ref[p, c]) * x0_ref[p, c]
        for j in range(r):
            o_ref[p, :, j * LANES:(j + 1) * LANES] = y_ref[p, pl.ds(j, ct, stride=r), :]


def _hy_longconv(z4, x0c4, kr, ki, bias, tabs, ct):
    b, d, r, _ = z4.shape
    n1 = tabs["n1"]
    const = lambda a: pl.BlockSpec(a.shape, lambda i, j, nd=a.ndim: (0,) * nd)
    data = pl.BlockSpec((2, ct, r, LANES), lambda i, j: (j, i, 0, 0))
    spec = pl.BlockSpec((ct, n1, LANES), lambda i, j: (i, 0, 0))
    consts = [tabs["m1_cplx"], tabs["m1_inv"], tabs["tw"], tabs["w2_fwd"], tabs["w2_inv"]]
    return pl.pallas_call(
        functools.partial(_hy_longconv_kernel, ct=ct, n1=n1, r=r),
        grid=(d // ct, b // 2),
        in_specs=[data, data, spec, spec, pl.BlockSpec((ct, 1, 1), lambda i, j: (i, 0, 0))]
                 + [const(a) for a in consts],
        out_specs=pl.BlockSpec((2, ct, r * LANES), lambda i, j: (j, i, 0)),
        out_shape=jax.ShapeDtypeStruct((b, d, r * LANES), F32),
        scratch_shapes=[pltpu.VMEM((2, ct * r, LANES), F32)],
        compiler_params=_params("parallel", "parallel"),
        name="hy_longconv",
    )(z4, x0c4, kr, ki, bias.reshape(d, 1, 1), *consts)


def _router_tail(x1, m, gf_ref, wrt_ref, x1_ref, hf_ref, aff_ref):
    x1_ref[0] = x1
    hf = _norm_mod(x1, gf_ref[...], m[3:4], m[4:5]).astype(BF16)
    hf_ref[0] = hf
    logits = _dot_nt(wrt_ref[...], hf)
    ex = jnp.exp(logits - jnp.max(logits, axis=0, keepdims=True))
    aff_ref[0] = ex / jnp.sum(ex, axis=0, keepdims=True)


def _hy_outproj_kernel(y_ref, x_ref, m_ref, w_ref, b_ref, gf_ref, wrt_ref, x1_ref, hf_ref, aff_ref):
    m = m_ref[0]
    y = jnp.transpose(y_ref[0]).astype(BF16)
    x1 = x_ref[0] + m[2:3] * (_dot(y, w_ref[...]) + b_ref[...])
    _router_tail(x1, m, gf_ref, wrt_ref, x1_ref, hf_ref, aff_ref)


def _mla_outproj_kernel(y_ref, x_ref, m_ref, w_ref, gf_ref, wrt_ref, x1_ref, hf_ref, aff_ref):
    m = m_ref[0]
    x1 = x_ref[0] + m[2:3] * _dot(y_ref[0].astype(BF16), w_ref[...])
    _router_tail(x1, m, gf_ref, wrt_ref, x1_ref, hf_ref, aff_ref)


def _outproj(y, x, mod, w_out, b_out, g_ffn, w_router_t, tt, channel_major):
    b, s, d = x.shape
    e = w_router_t.shape[0]
    dy = w_out.shape[0]
    tile = pl.BlockSpec((1, tt, d), lambda i, j: (i, j, 0))
    const = lambda shape: pl.BlockSpec(shape, lambda i, j: (0, 0))
    if channel_major:
        y_spec = pl.BlockSpec((1, dy, tt), lambda i, j: (i, 0, j))
        kern, extra, extra_specs = _hy_outproj_kernel, [b_out.reshape(1, d)], [const((1, d))]
    else:
        y_spec = pl.BlockSpec((1, tt, dy), lambda i, j: (i, j, 0))
        kern, extra, extra_specs = _mla_outproj_kernel, [], []
    return pl.pallas_call(
        kern,
        grid=(b, s // tt),
        in_specs=[y_spec, tile, pl.BlockSpec((1, 6, d), lambda i, j: (i, 0, 0)), const((dy, d))]
                 + extra_specs + [const((1, d)), const((e, d))],
        out_specs=[tile, tile, pl.BlockSpec((1, e, tt), lambda i, j: (i, 0, j))],
        out_shape=[jax.ShapeDtypeStruct((b, s, d), F32), jax.ShapeDtypeStruct((b, s, d), BF16),
                   jax.ShapeDtypeStruct((b, e, s), F32)],
        compiler_params=_params("parallel", "parallel"),
        name="outproj_router",
    )(y, x, mod, w_out, *extra, g_ffn.reshape(1, d), w_router_t)


def _route_kernel(a_ref, o_ref, idx_ref, cs_ref, *, n_exp, n_chunk, cap):
    rows = n_exp * n_chunk
    a = a_ref[0].reshape(rows, LANES)
    ri = lax.broadcasted_iota(I32, (rows, rows), 0)
    rj = lax.broadcasted_iota(I32, (rows, rows), 1)
    same = (ri // n_chunk) == (rj // n_chunk)
    blk_all = jnp.where(same, 1.0, 0.0).astype(BF16)
    blk_before = jnp.where(same & (rj < ri), 1.0, 0.0).astype(BF16)
    li = lax.broadcasted_iota(I32, (LANES, LANES), 0)
    lj = lax.broadcasted_iota(I32, (LANES, LANES), 1)
    ones = jnp.ones((LANES, LANES), BF16)
    before = jnp.where(li < lj, 1.0, 0.0).astype(BF16)

    def as_bf16(mask):
        return jnp.where(mask, 1.0, 0.0).astype(BF16)

    def expert_count(mask):
        return _dot(blk_all, _dot(as_bf16(mask), ones).astype(BF16))

    def prefix(mask):
        mb = as_bf16(mask)
        return _dot(mb, before) + _dot(blk_before, _dot(mb, ones).astype(BF16))

    def enough(threshold):
        return expert_count(a >= threshold) >= cap

    cur = jnp.zeros((rows, LANES), I32)
    for bit in range(30, 0, -2):
        b1, b0 = 1 << bit, 1 << (bit - 1)
        c1, c0, c10 = cur | b1, cur | b0, cur | (b1 | b0)
        t1, t0, t10 = (enough(pltpu.bitcast(c, F32)) for c in (c1, c0, c10))
        cur = jnp.where(t10, c10, jnp.where(t1, c1, jnp.where(t0, c0, cur)))
    cur = jnp.where(enough(pltpu.bitcast(cur | 1, F32)), cur | 1, cur)
    lo = pltpu.bitcast(cur, F32)
    hi = pltpu.bitcast(cur + 1, F32)
    for _ in range(2):
        width = hi - lo
        m1, m2, m3 = lo + width * 0.25, lo + width * 0.5, lo + width * 0.75
        t1, t2, t3 = enough(m1), enough(m2), enough(m3)
        lo, hi = (jnp.where(t3, m3, jnp.where(t2, m2, jnp.where(t1, m1, lo))),
                  jnp.where(t1, jnp.where(t2, jnp.where(t3, hi, m3), m2), m1))
    gt = a >= hi
    eq = (a >= lo) & (a < hi)
    need = cap - expert_count(gt)
    sel = gt | (eq & (prefix(eq) < need))
    sel_b = as_bf16(sel)
    chunk_total = _dot(sel_b, ones)
    chunk_start = _dot(blk_before, chunk_total.astype(BF16))
    within = _dot(sel_b, before)
    o_ref[0] = jnp.where(sel, (chunk_start + within).astype(I32), -1).reshape(n_exp, n_chunk, LANES)
    cs_ref[0] = chunk_start[:, 0:1].astype(I32).reshape(n_exp, n_chunk, 1)

    rank = jnp.where(sel, within + 1.0, 0.0).astype(BF16)
    start_row = jnp.transpose(chunk_start)[0:1, :]
    end_row = jnp.transpose(chunk_start + chunk_total)[0:1, :]
    s_col = lax.broadcasted_iota(I32, (cap, rows), 0).astype(F32)
    row_id = lax.broadcasted_iota(I32, (cap, rows), 1)
    lane_f = lax.broadcasted_iota(I32, (cap, LANES), 1).astype(F32)
    for ex in range(n_exp):
        in_chunk = ((row_id // n_chunk) == ex) & (start_row <= s_col) & (s_col < end_row)
        rank_rows = _dot(jnp.where(in_chunk, 1.0, 0.0).astype(BF16), rank)
        start_s = jnp.sum(jnp.where(in_chunk, start_row, 0.0), axis=1, keepdims=True)
        chunk_s = jnp.sum(jnp.where(in_chunk, (row_id - ex * n_chunk).astype(F32), 0.0), axis=1, keepdims=True)
        match = rank_rows == (s_col[:, 0:1] + 1.0 - start_s)
        lane_s = jnp.sum(jnp.where(match, lane_f, 0.0), axis=1, keepdims=True)
        idx_ref[0, ex] = (chunk_s * LANES + lane_s).astype(I32)


def _route(aff_t, cap):
    b, e, t = aff_t.shape
    nc = t // LANES
    blk = pl.BlockSpec((1, e, nc, LANES), lambda i: (i, 0, 0, 0))
    slot, idx, chunk_start = pl.pallas_call(
        functools.partial(_route_kernel, n_exp=e, n_chunk=nc, cap=cap),
        grid=(b,),
        in_specs=[blk],
        out_specs=[blk, pl.BlockSpec((1, e, cap, 1), lambda i: (i, 0, 0, 0)),
                   pl.BlockSpec((1, e, nc, 1), lambda i: (i, 0, 0, 0))],
        out_shape=[jax.ShapeDtypeStruct((b, e, nc, LANES), I32), jax.ShapeDtypeStruct((b, e, cap, 1), I32),
                   jax.ShapeDtypeStruct((b, e, nc, 1), I32)],
        compiler_params=_params("parallel"),
        name="route_topc",
    )(aff_t.reshape(b, e, nc, LANES))
    return slot.reshape(b, e, t), idx.reshape(b, e, cap), chunk_start.reshape(b, e, nc)


SCATTER_UNROLL = 8


def _expert_kernel(idx_ref, cs_ref, hf_ref, slot_ref, aff_ref, wg_ref, wu_ref, wd_ref, acc_ref, ye_ref, xs_ref,
                   gsl_ref, *, cap, tk, win):
    t, d = hf_ref.shape[1], hf_ref.shape[2]
    nc = d // LANES

    @pl.when(pl.program_id(1) == 0)
    def _():
        acc_ref[...] = jnp.zeros_like(acc_ref)

    xs_ref[...] = jnp.zeros_like(xs_ref)
    gsl_ref[...] = jnp.zeros_like(gsl_ref)
    n_tk = t // tk
    for k in range(n_tk):
        cols = slice(k * tk, (k + 1) * tk)

        def gather(n_rows, first):
            hit = (lax.broadcasted_iota(I32, (n_rows, tk), 0) + first) == slot_ref[0, 0, :, cols]
            x = _dot(jnp.where(hit, 1.0, 0.0).astype(BF16), hf_ref[0, cols, :])
            g = jnp.sum(jnp.where(hit, aff_ref[0, 0, :, cols], 0.0), axis=1, keepdims=True)
            return x, g

        lo = cs_ref[0, 0, 0, k * (tk // LANES)]
        hi = cs_ref[0, 0, 0, (k + 1) * (tk // LANES)] if k + 1 < n_tk else cap
        first = jnp.minimum((lo // 8) * 8, cap - win)
        fits = (hi - first) <= win

        @pl.when(fits)
        def _():
            x, g = gather(win, first)
            rows = pl.ds(pl.multiple_of(first, 8), win)
            xs_ref[rows, :] += x
            gsl_ref[rows, :] += g

        @pl.when(jnp.logical_not(fits))
        def _():
            x, g = gather(cap, 0)
            xs_ref[...] += x
            gsl_ref[...] += g

    xs = xs_ref[...]
    gsl = gsl_ref[...]
    xs = xs.astype(BF16)
    gate = _dot(xs, wg_ref[0])
    up = _dot(xs, wu_ref[0])
    hid = gate / (1.0 + jnp.exp(-gate)) * up
    ye = _dot(hid.astype(BF16), wd_ref[0]) * gsl
    for c in range(nc):
        ye_ref[pl.ds(c, cap, stride=nc), :] = ye[:, c * LANES:(c + 1) * LANES]

    def tile_of(i):
        return pl.ds(pl.multiple_of(i * nc, nc), nc)

    def scatter_group(g, carry):
        base = g * SCATTER_UNROLL
        toks = [idx_ref[0, 0, 0, base + u] for u in range(SCATTER_UNROLL)]
        sums = [acc_ref[0, tile_of(toks[u]), :] + ye_ref[tile_of(base + u), :] for u in range(SCATTER_UNROLL)]
        for u in range(SCATTER_UNROLL):
            acc_ref[0, tile_of(toks[u]), :] = sums[u]
        return carry

    lax.fori_loop(0, cap // SCATTER_UNROLL, scatter_group, 0)


def _experts(hf, idx, chunk_start, slot_row, aff_row, wg, wu, wd, cap):
    b, t, d = hf.shape
    e, _, f = wg.shape
    nc = d // LANES
    tk = max(LANES, t // 4)
    win = max(8, cap // 2)
    row = pl.BlockSpec((1, 1, 1, t), lambda i, j: (i, j, 0, 0))
    smem_row = lambda n: pl.BlockSpec((1, 1, 1, n), lambda i, j: (i, j, 0, 0), memory_space=pltpu.SMEM)
    once = pl.Buffered(1)
    return pl.pallas_call(
        functools.partial(_expert_kernel, cap=cap, tk=tk, win=win),
        grid=(b, e),
        in_specs=[smem_row(cap), smem_row(t // LANES),
                  pl.BlockSpec((1, t, d), lambda i, j: (i, 0, 0), pipeline_mode=once), row, row,
                  pl.BlockSpec((1, d, f), lambda i, j: (j, 0, 0)),
                  pl.BlockSpec((1, d, f), lambda i, j: (j, 0, 0)),
                  pl.BlockSpec((1, f, d), lambda i, j: (j, 0, 0))],
        out_specs=pl.BlockSpec((1, t * nc, LANES), lambda i, j: (i, 0, 0), pipeline_mode=once),
        out_shape=jax.ShapeDtypeStruct((b, t * nc, LANES), F32),
        scratch_shapes=[pltpu.VMEM((cap * nc, LANES), F32), pltpu.VMEM((cap, d), F32), pltpu.VMEM((cap, 1), F32)],
        compiler_params=_params("parallel", "arbitrary"),
        name="expert_ffn",
    )(idx.reshape(b, e, 1, cap), chunk_start.reshape(b, e, 1, t // LANES), hf, slot_row, aff_row, wg, wu, wd)


def _combine_kernel(acc_ref, x_ref, m_ref, gn_ref, o_ref, *, final_norm):
    tt, d = o_ref.shape[1], o_ref.shape[2]
    nc = d // LANES
    moe = jnp.concatenate([acc_ref[0, pl.ds(c, tt, stride=nc), :] for c in range(nc)], axis=1)
    x2 = x_ref[0] + m_ref[0][5:6] * moe
    o_ref[0] = _rms(x2, gn_ref[...]) if final_norm else x2


def _combine(acc, x1, mod, g_final, tt, final_norm):
    b, t, d = x1.shape
    nc = d // LANES
    tile = pl.BlockSpec((1, tt, d), lambda i, j: (i, j, 0))
    return pl.pallas_call(
        functools.partial(_combine_kernel, final_norm=final_norm),
        grid=(b, t // tt),
        in_specs=[pl.BlockSpec((1, tt * nc, LANES), lambda i, j: (i, j, 0)),
                  tile, pl.BlockSpec((1, 6, d), lambda i, j: (i, 0, 0)),
                  pl.BlockSpec((1, d), lambda i, j: (0, 0))],
        out_specs=tile,
        out_shape=jax.ShapeDtypeStruct((b, t, d), F32),
        compiler_params=_params("parallel", "parallel"),
        name="moe_combine",
    )(acc, x1, mod, g_final.reshape(1, d))


def _moe(hf, aff_t, x1, mod, wg, wu, wd, g_final, final_norm):
    b, t, d = hf.shape
    e = aff_t.shape[1]
    cap = EC_CAPACITY_FACTOR * t // e
    slot, idx, chunk_start = _route(aff_t, cap)
    acc = _experts(hf, idx, chunk_start, slot.reshape(b, e, 1, t), aff_t.reshape(b, e, 1, t), wg, wu, wd, cap)
    return _combine(acc, x1, mod, g_final, min(512, t), final_norm)


def _rope(x, cos_t, sin_lo, sin_hi):
    return x * cos_t + pltpu.roll(x, LANES - QK_ROPE // 2, 1) * sin_lo + pltpu.roll(x, QK_ROPE // 2, 1) * sin_hi


def _mla_proj_kernel(x_ref, m_ref, g_ref, pos_ref, invf_ref, win_ref, qg_ref, wq_ref, kg_ref, wkn_ref, wv_ref,
                     q_ref, k_ref, v_ref):
    m = m_ref[0]
    h = _norm_mod(x_ref[0], g_ref[...], m[0:1], m[1:2]).astype(BF16)
    a = _dot(h, win_ref[...])
    cq = a[:, :Q_LORA]
    ckv = a[:, Q_LORA:Q_LORA + KV_LORA]
    kpe = a[:, Q_LORA + KV_LORA:]
    q = _dot(_rms(cq, qg_ref[...]).astype(BF16), wq_ref[...])
    kvn = _rms(ckv, kg_ref[...]).astype(BF16)
    kn = _dot(kvn, wkn_ref[...])
    v_ref[0] = _dot(kvn, wv_ref[...]).astype(BF16)
    ang = pos_ref[0] * invf_ref[...]
    lane = lax.broadcasted_iota(I32, ang.shape, 1)
    cos_t = jnp.where(lane < QK_ROPE, jnp.cos(ang), 0.0)
    sin_a = jnp.sin(ang)
    sin_lo = jnp.where(lane < QK_ROPE // 2, -sin_a, 0.0)
    sin_hi = jnp.where((lane >= QK_ROPE // 2) & (lane < QK_ROPE), sin_a, 0.0)
    kpe = _rope(kpe, cos_t, sin_lo, sin_hi)
    qs, ks = [], []
    for hd in range(MLA_HEADS):
        qs.append(q[:, hd * HEAD_PAD:hd * HEAD_PAD + QK_NOPE])
        qs.append(_rope(q[:, hd * HEAD_PAD + QK_NOPE:(hd + 1) * HEAD_PAD], cos_t, sin_lo, sin_hi))
        ks.append(kn[:, hd * QK_NOPE:(hd + 1) * QK_NOPE])
        ks.append(kpe)
    q_ref[0] = jnp.concatenate(qs, axis=1).astype(BF16)
    k_ref[0] = jnp.concatenate(ks, axis=1).astype(BF16)


def _mla_proj(x, mod, g, positions, w_in, q_g, w_qb, kv_g, w_kvb, tt):
    b, s, d = x.shape
    nh = MLA_HEADS
    pad = HEAD_PAD - QK_NOPE - QK_ROPE
    win = jnp.concatenate([w_in, jnp.zeros((d, LANES - QK_ROPE), F32)], axis=1).astype(BF16)
    wq = w_qb.reshape(Q_LORA, nh, QK_NOPE + QK_ROPE)
    wq = jnp.concatenate([wq, jnp.zeros((Q_LORA, nh, pad), F32)], axis=2).reshape(Q_LORA, nh * HEAD_PAD).astype(BF16)
    wkv = w_kvb.reshape(KV_LORA, nh, QK_NOPE + V_HEAD)
    wkn = wkv[:, :, :QK_NOPE].reshape(KV_LORA, nh * QK_NOPE).astype(BF16)
    wv = wkv[:, :, QK_NOPE:].reshape(KV_LORA, nh * V_HEAD).astype(BF16)
    inv_freq = ROPE_THETA ** (-jnp.arange(0, QK_ROPE, 2, dtype=F32) / QK_ROPE)
    invf = jnp.concatenate([inv_freq, inv_freq, jnp.zeros((LANES - QK_ROPE,), F32)]).reshape(1, LANES)
    pos = positions.astype(F32).reshape(b, s, 1)
    const = lambda a: pl.BlockSpec(a.shape, lambda i, j: (0, 0))
    tile = lambda w: pl.BlockSpec((1, tt, w), lambda i, j: (i, j, 0))
    g2, qg2, kg2 = g.reshape(1, d), q_g.reshape(1, Q_LORA), kv_g.reshape(1, KV_LORA)
    return pl.pallas_call(
        _mla_proj_kernel,
        grid=(b, s // tt),
        in_specs=[tile(d), pl.BlockSpec((1, 6, d), lambda i, j: (i, 0, 0)), const(g2), tile(1), const(invf),
                  const(win), const(qg2), const(wq), const(kg2), const(wkn), const(wv)],
        out_specs=[tile(nh * HEAD_PAD), tile(nh * HEAD_PAD), tile(nh * V_HEAD)],
        out_shape=[jax.ShapeDtypeStruct((b, s, nh * HEAD_PAD), BF16),
                   jax.ShapeDtypeStruct((b, s, nh * HEAD_PAD), BF16),
                   jax.ShapeDtypeStruct((b, s, nh * V_HEAD), BF16)],
        compiler_params=_params("parallel", "parallel"),
        name="mla_proj",
    )(x, mod, g2, pos, invf, win, qg2, wq, kg2, wkn, wv)


def _mla_attn_kernel(q_ref, k_ref, v_ref, o_ref, *, n_sub):
    sub = q_ref.shape[1] // n_sub
    for i in range(n_sub):
        s = _dot_nt(q_ref[0, i * sub:(i + 1) * sub, :], k_ref[0])
        p = jnp.exp2((s - jnp.max(s, axis=1, keepdims=True)) * (SOFTMAX_SCALE * math.log2(math.e)))
        o = _dot(p.astype(BF16), v_ref[0])
        o_ref[0, i * sub:(i + 1) * sub, :] = o / jnp.sum(p, axis=1, keepdims=True)


def _mla_attn(q, k, v, tq):
    b, s, _ = q.shape
    nh = MLA_HEADS
    return pl.pallas_call(
        functools.partial(_mla_attn_kernel, n_sub=max(1, tq // 256)),
        grid=(b, nh, s // tq),
        in_specs=[pl.BlockSpec((1, tq, HEAD_PAD), lambda i, h, j: (i, j, h)),
                  pl.BlockSpec((1, s, HEAD_PAD), lambda i, h, j: (i, 0, h)),
                  pl.BlockSpec((1, s, V_HEAD), lambda i, h, j: (i, 0, h))],
        out_specs=pl.BlockSpec((1, tq, V_HEAD), lambda i, h, j: (i, j, h)),
        out_shape=jax.ShapeDtypeStruct((b, s, nh * V_HEAD), F32),
        compiler_params=_params("parallel", "parallel", "parallel"),
        name="mla_attn",
    )(q, k, v)


def kernel(x, c, positions, ada_w, ada_b, norm_mix_g, norm_ffn_g, hy_w_in, hy_b_in, hy_conv_w, hy_conv_b, hy_f_w1, hy_f_b1, hy_f_w2, hy_f_b2, hy_f_w3, hy_f_freq, hy_f_bias, hy_w_out, hy_b_out, mla_w_in, mla_q_norm_g, mla_w_qb, mla_kv_norm_g, mla_w_kvb, mla_w_out, moe_w_router, moe_w_gate, moe_w_up, moe_w_down, final_norm_g):
    b, s, d = x.shape
    depth = ada_w.shape[0]
    assert s % LANES == 0 and b % 2 == 0 and depth == 2
    tt = min(512, s)
    mod = _ada(c, ada_w, ada_b).reshape(depth, b, 6, d)

    u_t = _hy_inproj(x, mod[0], norm_mix_g[0], hy_w_in[0].T.astype(BF16), hy_b_in[0], tt)
    cwb = jnp.concatenate([hy_conv_w[0].T, hy_conv_b[0][:, None]], axis=1)
    z4, x0c4 = _hy_conv_gate(u_t, cwb, d, min(64, d))
    tabs = _dft_tables(s)
    n1, r = tabs["n1"], tabs["r"]
    k_t = _hy_filter(hy_f_w1[0], hy_f_b1[0], hy_f_w2[0], hy_f_b2[0], hy_f_w3[0], hy_f_freq[0], d, s)
    ct = min(16, d)
    kr, ki = _hy_filter_fft(k_t.reshape(d, n1, LANES), tabs, ct)
    yg = _hy_longconv(z4, x0c4, kr, ki, hy_f_bias[0, 0], tabs, ct)
    x1, hf, aff_t = _outproj(yg, x, mod[0], hy_w_out[0].astype(BF16), hy_b_out[0], norm_ffn_g[0],
                             moe_w_router[0].T.astype(BF16), tt, True)
    x2 = _moe(hf, aff_t, x1, mod[0], moe_w_gate[0].astype(BF16), moe_w_up[0].astype(BF16),
              moe_w_down[0].astype(BF16), final_norm_g, False)

    q, k, v = _mla_proj(x2, mod[1], norm_mix_g[1], positions, mla_w_in[0], mla_q_norm_g[0], mla_w_qb[0],
                        mla_kv_norm_g[0], mla_w_kvb[0], tt)
    o = _mla_attn(q, k, v, min(1024, s))
    x3, hf, aff_t = _outproj(o, x2, mod[1], mla_w_out[0].astype(BF16), None, norm_ffn_g[1],
                             moe_w_router[1].T.astype(BF16), tt, False)
    return _moe(hf, aff_t, x3, mod[1], moe_w_gate[1].astype(BF16), moe_w_up[1].astype(BF16),
                moe_w_down[1].astype(BF16), final_norm_g, True)
```

```python
import functools
import math

import numpy as np
import jax
import jax.numpy as jnp
from jax import lax
from jax.experimental import pallas as pl
from jax.experimental.pallas import tpu as pltpu

F32 = jnp.float32
BF16 = jnp.bfloat16
I32 = jnp.int32

EPS = 1e-6
LANES = 128
VMEM_LIMIT = 56 * 1024 * 1024

HY_BANDS = 16
HY_FAST_DECAY_PCT = 0.3
HY_SLOW_DECAY_PCT = 1.5
HY_DECAY_TARGET = 1e-2

MLA_HEADS = 8
QK_NOPE = 128
QK_ROPE = 64
V_HEAD = 128
Q_LORA = 256
KV_LORA = 128
ROPE_THETA = 10000.0
SOFTMAX_SCALE = 1.0 / math.sqrt(QK_NOPE + QK_ROPE)
HEAD_PAD = 256

EC_CAPACITY_FACTOR = 2

NT_DIMS = (((1,), (1,)), ((), ()))


def _params(*sem):
    return pltpu.CompilerParams(dimension_semantics=sem, vmem_limit_bytes=VMEM_LIMIT)


def _dot(a, b):
    return jnp.dot(a, b, preferred_element_type=F32)


def _dot_nt(a, b):
    return lax.dot_general(a, b, NT_DIMS, preferred_element_type=F32)


def _split(a):
    hi = a.astype(BF16)
    lo = (a - hi.astype(F32)).astype(BF16)
    return hi, lo


def _dot3(a, b):
    ah, al = _split(a)
    bh, bl = _split(b)
    return _dot(ah, bh) + (_dot(ah, bl) + _dot(al, bh))


def _dot1(a, b):
    return _dot(a.astype(BF16), b.astype(BF16))


def _rms(x, g):
    return x * lax.rsqrt(jnp.mean(x * x, axis=-1, keepdims=True) + EPS) * g


def _norm_mod(x, g, shift, scale):
    return _rms(x, g) * (1.0 + scale) + shift


def _ada_kernel(c_ref, w_ref, b_ref, o_ref):
    c = c_ref[...]
    cs = c / (1.0 + jnp.exp(-c))
    o_ref[0] = _dot(cs.astype(BF16), w_ref[0].astype(BF16)) + b_ref[0]


def _ada(c, ada_w, ada_b):
    depth, d, n = ada_w.shape
    b = c.shape[0]
    tn = n // 4
    return pl.pallas_call(
        _ada_kernel,
        grid=(depth, n // tn),
        in_specs=[pl.BlockSpec((b, d), lambda i, j: (0, 0)),
                  pl.BlockSpec((1, d, tn), lambda i, j: (i, 0, j)),
                  pl.BlockSpec((1, 1, tn), lambda i, j: (i, 0, j))],
        out_specs=pl.BlockSpec((1, b, tn), lambda i, j: (i, 0, j)),
        out_shape=jax.ShapeDtypeStruct((depth, b, n), F32),
        compiler_params=_params("parallel", "parallel"),
        name="ada_mod",
    )(c, ada_w, ada_b.reshape(depth, 1, n))


def _hy_inproj_kernel(x_ref, m_ref, g_ref, wt_ref, b_ref, o_ref):
    m = m_ref[0]
    h = _norm_mod(x_ref[0], g_ref[...], m[0:1], m[1:2])
    o_ref[0] = _dot_nt(wt_ref[...], h.astype(BF16)) + b_ref[...]


def _hy_inproj(x, mod, g, w_in_t, b_in, tt):
    b, s, d = x.shape
    n = w_in_t.shape[0]
    return pl.pallas_call(
        _hy_inproj_kernel,
        grid=(b, s // tt),
        in_specs=[pl.BlockSpec((1, tt, d), lambda i, j: (i, j, 0)),
                  pl.BlockSpec((1, 6, d), lambda i, j: (i, 0, 0)),
                  pl.BlockSpec((1, d), lambda i, j: (0, 0)),
                  pl.BlockSpec((n, d), lambda i, j: (0, 0)),
                  pl.BlockSpec((n, 1), lambda i, j: (0, 0))],
        out_specs=pl.BlockSpec((1, n, tt), lambda i, j: (i, 0, j)),
        out_shape=jax.ShapeDtypeStruct((b, n, s), F32),
        compiler_params=_params("parallel", "parallel"),
        name="hy_inproj",
    )(x, mod, g.reshape(1, d), w_in_t, b_in.reshape(n, 1))


def _short_conv(u, w):
    length = u.shape[1]
    lane = lax.broadcasted_iota(I32, (u.shape[0], LANES), 1)
    prev = pltpu.roll(u, 1, 1)
    nxt = pltpu.roll(u, length - 1, 1)
    prev = jnp.concatenate([jnp.where(lane == 0, 0.0, prev[:, :LANES]), prev[:, LANES:]], axis=1)
    nxt = jnp.concatenate([nxt[:, :length - LANES], jnp.where(lane == LANES - 1, 0.0, nxt[:, length - LANES:])], axis=1)
    return w[:, 0:1] * prev + w[:, 1:2] * u + w[:, 2:3] * nxt + w[:, 3:4]


def _hy_conv_gate_kernel(x0_ref, x1_ref, v_ref, w0_ref, w1_ref, wv_ref, z_ref, x0c_ref, *, group):
    ct, s = v_ref.shape[1], v_ref.shape[2]
    r = s // LANES
    z = _short_conv(v_ref[0], wv_ref[...]) * _short_conv(x1_ref[0], w1_ref[...])
    x0c = _short_conv(x0_ref[0], w0_ref[...])
    for g in range(ct // group):
        for j in range(r):
            rows = slice((g * r + j) * group, (g * r + j + 1) * group)
            chans = slice(g * group, (g + 1) * group)
            z_ref[0, rows, :] = z[chans, j * LANES:(j + 1) * LANES]
            x0c_ref[0, rows, :] = x0c[chans, j * LANES:(j + 1) * LANES]


def _hy_conv_gate(u_t, cwb, d, ct, group):
    b, _, s = u_t.shape
    nb = d // ct
    r = s // LANES
    u_spec = lambda p: pl.BlockSpec((1, ct, s), lambda i, j, p=p: (i, p * nb + j, 0))
    w_spec = lambda p: pl.BlockSpec((ct, 4), lambda i, j, p=p: (p * nb + j, 0))
    out_spec = pl.BlockSpec((1, ct * r, LANES), lambda i, j: (i, j, 0))
    return pl.pallas_call(
        functools.partial(_hy_conv_gate_kernel, group=group),
        grid=(b, nb),
        in_specs=[u_spec(0), u_spec(1), u_spec(2), w_spec(0), w_spec(1), w_spec(2)],
        out_specs=[out_spec, out_spec],
        out_shape=[jax.ShapeDtypeStruct((b, d * r, LANES), F32)] * 2,
        compiler_params=_params("parallel", "parallel"),
        name="hy_conv_gate",
    )(u_t, u_t, u_t, cwb, cwb, cwb)


def _hy_filter_kernel(w1t_ref, b1_ref, w2t_ref, b2_ref, w3t_ref, fr_ref, o_ref, *, length, lt, nfeat):
    half = pl.program_id(0)
    j = pl.program_id(1)
    d = o_ref.shape[0]
    rows = w1t_ref.shape[1]
    hi = lax.Precision.HIGHEST
    t = half * length + j * lt + lax.broadcasted_iota(I32, (rows, lt), 1)
    pos = jnp.where(half == 0, t, 2 * length - t).astype(F32)
    r = lax.broadcasted_iota(I32, (rows, lt), 0)
    tlin = pos / (length - 1.0)
    w = 2.0 * math.pi * pos / length
    fidx = jnp.where(r > HY_BANDS, r - (HY_BANDS + 1), r - 1).astype(F32)
    f = 1e-4 + fidx * ((HY_BANDS - 1 - 1e-4) / (HY_BANDS - 1))
    ang = f * w
    feat = jnp.where(r == 0, tlin, jnp.where(r <= HY_BANDS, jnp.cos(ang), -jnp.sin(ang)))
    feat = jnp.where(r < nfeat, feat, 0.0)
    fr = fr_ref[...]
    h = jnp.sin(fr * (jnp.dot(w1t_ref[...], feat, precision=hi, preferred_element_type=F32) + b1_ref[...]))
    h = jnp.sin(fr * (jnp.dot(w2t_ref[...], h, precision=hi, preferred_element_type=F32) + b2_ref[...]))
    out = jnp.dot(w3t_ref[...], h, precision=hi, preferred_element_type=F32)
    c = lax.broadcasted_iota(I32, (d, lt), 0).astype(F32)
    min_decay = math.log(HY_DECAY_TARGET) / HY_FAST_DECAY_PCT
    max_decay = math.log(HY_DECAY_TARGET) / HY_SLOW_DECAY_PCT
    delta = min_decay + c * ((max_decay - min_decay) / (d - 1.0))
    decay = jnp.exp(-tlin[0:1, :] * jnp.abs(delta))
    valid = pos[0:1, :] < length
    o_ref[...] = jnp.where(valid, out * decay, 0.0)


def _hy_filter(f_w1, f_b1, f_w2, f_b2, f_w3, f_freq, d, length):
    nfeat, hid = f_w1.shape
    rows = ((nfeat + 7) // 8) * 8
    w1t = jnp.zeros((hid, rows), F32).at[:, :nfeat].set(f_w1.T)
    lt = min(512, length)
    col = lambda a: a.reshape(hid, 1)
    small = lambda shape: pl.BlockSpec(shape, lambda h, j: (0, 0))
    return pl.pallas_call(
        functools.partial(_hy_filter_kernel, length=length, lt=lt, nfeat=nfeat),
        grid=(2, length // lt),
        in_specs=[small((hid, rows)), small((hid, 1)), small((hid, hid)), small((hid, 1)),
                  pl.BlockSpec((d, hid), lambda h, j: (h, 0)), small((hid, 1))],
        out_specs=pl.BlockSpec((d, lt), lambda h, j: (0, h * (length // lt) + j)),
        out_shape=jax.ShapeDtypeStruct((d, 2 * length), F32),
        compiler_params=_params("parallel", "parallel"),
        name="hy_filter",
    )(w1t, col(f_b1), f_w2.T, col(f_b2), f_w3.T, col(f_freq))


def _dft_tables(length):
    n = 2 * length
    n1 = n // LANES
    r = n1 // 2
    k1 = np.arange(n1)[:, None]
    ang1 = 2.0 * np.pi * k1 * np.arange(n1)[None, :] / n1
    c1, s1 = np.cos(ang1), np.sin(ang1)
    m1_real = np.concatenate([c1, -s1], axis=0)
    m1_cplx = np.block([[c1[:, :r], s1[:, :r]], [-s1[:, :r], c1[:, :r]]])
    m1_inv = np.block([[c1[:r, :], -s1[:r, :]], [s1[:r, :], c1[:r, :]]])
    ang_t = 2.0 * np.pi * k1 * np.arange(LANES)[None, :] / n
    tw = np.stack([np.cos(ang_t), np.sin(ang_t)])
    n2 = np.arange(LANES)
    ang2 = 2.0 * np.pi * n2[:, None] * n2[None, :] / LANES
    c2, s2 = np.cos(ang2), np.sin(ang2)
    w2_fwd = np.block([[c2, -s2], [s2, c2]])
    w2_inv = np.block([[c2, s2], [-s2, c2]])
    f = lambda a: jnp.asarray(a, F32)
    return dict(n1=n1, r=r, m1_real=f(m1_real), m1_cplx=f(m1_cplx), m1_inv=f(m1_inv),
                tw=f(tw), w2_fwd=f(w2_fwd), w2_inv=f(w2_inv))


def _wide(ref3, ct):
    return jnp.concatenate([ref3[c] for c in range(ct)], axis=1)


def _fft_fwd(x_wide, m1, tc, ts, w2, ct, n1, mm):
    a = mm(m1, x_wide)
    ar, ai = a[:n1], a[n1:]
    tall = []
    for c in range(ct):
        arc, aic = ar[:, c * LANES:(c + 1) * LANES], ai[:, c * LANES:(c + 1) * LANES]
        tall.append(jnp.concatenate([arc * tc + aic * ts, aic * tc - arc * ts], axis=1))
    return mm(jnp.concatenate(tall, axis=0), w2)


def _fft_inv(y_tall, w2i, tc, ts, m1i, ct, n1, mm):
    cc = mm(y_tall, w2i)
    dr, di = [], []
    for c in range(ct):
        cr, ci = cc[c * n1:(c + 1) * n1, :LANES], cc[c * n1:(c + 1) * n1, LANES:]
        dr.append(cr * tc - ci * ts)
        di.append(ci * tc + cr * ts)
    d_wide = jnp.concatenate([jnp.concatenate(dr, axis=1), jnp.concatenate(di, axis=1)], axis=0)
    return mm(m1i, d_wide)


def _hy_filter_fft_kernel(k_ref, m1_ref, tw_ref, w2_ref, kr_ref, ki_ref, *, ct, n1):
    x = _fft_fwd(_wide(k_ref, ct), m1_ref[...], tw_ref[0], tw_ref[1], w2_ref[...], ct, n1, _dot3)
    x = x * (1.0 / (n1 * LANES))
    kr_ref[...] = x[:, :LANES].reshape(ct, n1, LANES)
    ki_ref[...] = x[:, LANES:].reshape(ct, n1, LANES)


def _hy_filter_fft(k4, tabs, ct):
    d, n1, _ = k4.shape
    const = lambda a: pl.BlockSpec(a.shape, lambda i, nd=a.ndim: (0,) * nd)
    blk = pl.BlockSpec((ct, n1, LANES), lambda i: (i, 0, 0))
    return pl.pallas_call(
        functools.partial(_hy_filter_fft_kernel, ct=ct, n1=n1),
        grid=(d // ct,),
        in_specs=[blk, const(tabs["m1_real"]), const(tabs["tw"]), const(tabs["w2_fwd"])],
        out_specs=[blk, blk],
        out_shape=[jax.ShapeDtypeStruct((d, n1, LANES), F32)] * 2,
        compiler_params=_params("parallel"),
        name="hy_filter_fft",
    )(k4, tabs["m1_real"], tabs["tw"], tabs["w2_fwd"])


def _hy_longconv_kernel(z_ref, x0_ref, kr_ref, ki_ref, bias_ref, m1_ref, m1i_ref, tw_ref, w2_ref, w2i_ref,
                        o_ref, y_ref, *, ct, n1, r):
    tc, ts = tw_ref[0], tw_ref[1]
    def chan(ref, p, c):
        return ref[p, pl.ds(c, r, stride=ct), :]

    x_wide = jnp.concatenate([jnp.concatenate([chan(z_ref, p, c) for c in range(ct)], axis=1) for p in range(2)],
                             axis=0)
    x = _fft_fwd(x_wide, m1_ref[...], tc, ts, w2_ref[...], ct, n1, _dot1)
    xr, xi = x[:, :LANES], x[:, LANES:]
    kr = kr_ref[...].reshape(ct * n1, LANES)
    ki = ki_ref[...].reshape(ct * n1, LANES)
    y_tall = jnp.concatenate([xr * kr - xi * ki, xr * ki + xi * kr], axis=1)
    y = _fft_inv(y_tall, w2i_ref[...], tc, ts, m1i_ref[...], ct, n1, _dot1)
    for p in range(2):
        for c in range(ct):
            conv = y[p * r:(p + 1) * r, c * LANES:(c + 1) * LANES]
            y_ref[p, c * r:(c + 1) * r, :] = (conv + bias_ref[c] * chan(z_ref, p, c)) * chan(x0_ref, p, c)
        for j in range(r):
            o_ref[p, :, j * LANES:(j + 1) * LANES] = y_ref[p, pl.ds(j, ct, stride=r), :]


def _hy_longconv(z, x0c, kr, ki, bias, tabs, ct):
    b, d = z.shape[0], bias.shape[0]
    n1, r = tabs["n1"], tabs["r"]
    const = lambda a: pl.BlockSpec(a.shape, lambda i, j, nd=a.ndim: (0,) * nd)
    data = pl.BlockSpec((2, ct * r, LANES), lambda i, j: (j, i, 0))
    spec = pl.BlockSpec((ct, n1, LANES), lambda i, j: (i, 0, 0))
    consts = [tabs["m1_cplx"], tabs["m1_inv"], tabs["tw"], tabs["w2_fwd"], tabs["w2_inv"]]
    return pl.pallas_call(
        functools.partial(_hy_longconv_kernel, ct=ct, n1=n1, r=r),
        grid=(d // ct, b // 2),
        in_specs=[data, data, spec, spec, pl.BlockSpec((ct, 1, 1), lambda i, j: (i, 0, 0))]
                 + [const(a) for a in consts],
        out_specs=pl.BlockSpec((2, ct, r * LANES), lambda i, j: (j, i, 0)),
        out_shape=jax.ShapeDtypeStruct((b, d, r * LANES), F32),
        scratch_shapes=[pltpu.VMEM((2, ct * r, LANES), F32)],
        compiler_params=_params("parallel", "parallel"),
        name="hy_longconv",
    )(z, x0c, kr, ki, bias.reshape(d, 1, 1), *consts)


def _router_tail(x1, m, gf_ref, wrt_ref, x1_ref, hf_ref, aff_ref):
    x1_ref[0] = x1
    hf = _norm_mod(x1, gf_ref[...], m[3:4], m[4:5]).astype(BF16)
    hf_ref[0] = hf
    logits = _dot_nt(wrt_ref[...], hf)
    ex = jnp.exp(logits - jnp.max(logits, axis=0, keepdims=True))
    aff_ref[0] = ex / jnp.sum(ex, axis=0, keepdims=True)


def _hy_outproj_kernel(y_ref, x_ref, m_ref, w_ref, b_ref, gf_ref, wrt_ref, x1_ref, hf_ref, aff_ref):
    m = m_ref[0]
    y = jnp.transpose(y_ref[0]).astype(BF16)
    x1 = x_ref[0] + m[2:3] * (_dot(y, w_ref[...]) + b_ref[...])
    _router_tail(x1, m, gf_ref, wrt_ref, x1_ref, hf_ref, aff_ref)


def _mla_outproj_kernel(y_ref, x_ref, m_ref, w_ref, gf_ref, wrt_ref, x1_ref, hf_ref, aff_ref):
    m = m_ref[0]
    x1 = x_ref[0] + m[2:3] * _dot(y_ref[0].astype(BF16), w_ref[...])
    _router_tail(x1, m, gf_ref, wrt_ref, x1_ref, hf_ref, aff_ref)


def _outproj(y, x, mod, w_out, b_out, g_ffn, w_router_t, tt, channel_major):
    b, s, d = x.shape
    e = w_router_t.shape[0]
    dy = w_out.shape[0]
    tile = pl.BlockSpec((1, tt, d), lambda i, j: (i, j, 0))
    const = lambda shape: pl.BlockSpec(shape, lambda i, j: (0, 0))
    if channel_major:
        y_spec = pl.BlockSpec((1, dy, tt), lambda i, j: (i, 0, j))
        kern, extra, extra_specs = _hy_outproj_kernel, [b_out.reshape(1, d)], [const((1, d))]
    else:
        y_spec = pl.BlockSpec((1, tt, dy), lambda i, j: (i, j, 0))
        kern, extra, extra_specs = _mla_outproj_kernel, [], []
    return pl.pallas_call(
        kern,
        grid=(b, s // tt),
        in_specs=[y_spec, tile, pl.BlockSpec((1, 6, d), lambda i, j: (i, 0, 0)), const((dy, d))]
                 + extra_specs + [const((1, d)), const((e, d))],
        out_specs=[tile, tile, pl.BlockSpec((1, e, tt), lambda i, j: (i, 0, j))],
        out_shape=[jax.ShapeDtypeStruct((b, s, d), F32), jax.ShapeDtypeStruct((b, s, d), BF16),
                   jax.ShapeDtypeStruct((b, e, s), F32)],
        compiler_params=_params("parallel", "parallel"),
        name="outproj_router",
    )(y, x, mod, w_out, *extra, g_ffn.reshape(1, d), w_router_t)


def _route_kernel(a_ref, o_ref, idx_ref, cs_ref, *, n_exp, n_chunk, cap):
    rows = n_exp * n_chunk
    a = a_ref[0].reshape(rows, LANES)
    ri = lax.broadcasted_iota(I32, (rows, rows), 0)
    rj = lax.broadcasted_iota(I32, (rows, rows), 1)
    same = (ri // n_chunk) == (rj // n_chunk)
    blk_all = jnp.where(same, 1.0, 0.0).astype(BF16)
    blk_before = jnp.where(same & (rj < ri), 1.0, 0.0).astype(BF16)
    li = lax.broadcasted_iota(I32, (LANES, LANES), 0)
    lj = lax.broadcasted_iota(I32, (LANES, LANES), 1)
    ones = jnp.ones((LANES, LANES), BF16)
    before = jnp.where(li < lj, 1.0, 0.0).astype(BF16)

    def as_bf16(mask):
        return jnp.where(mask, 1.0, 0.0).astype(BF16)

    def expert_count(mask):
        return _dot(blk_all, _dot(as_bf16(mask), ones).astype(BF16))

    def prefix(mask):
        mb = as_bf16(mask)
        return _dot(mb, before) + _dot(blk_before, _dot(mb, ones).astype(BF16))

    def enough(threshold):
        return expert_count(a >= threshold) >= cap

    cur = jnp.zeros((rows, LANES), I32)
    for bit in range(30, 0, -2):
        b1, b0 = 1 << bit, 1 << (bit - 1)
        c1, c0, c10 = cur | b1, cur | b0, cur | (b1 | b0)
        t1, t0, t10 = (enough(pltpu.bitcast(c, F32)) for c in (c1, c0, c10))
        cur = jnp.where(t10, c10, jnp.where(t1, c1, jnp.where(t0, c0, cur)))
    cur = jnp.where(enough(pltpu.bitcast(cur | 1, F32)), cur | 1, cur)
    lo = pltpu.bitcast(cur, F32)
    hi = pltpu.bitcast(cur + 1, F32)
    for _ in range(2):
        width = hi - lo
        m1, m2, m3 = lo + width * 0.25, lo + width * 0.5, lo + width * 0.75
        t1, t2, t3 = enough(m1), enough(m2), enough(m3)
        lo, hi = (jnp.where(t3, m3, jnp.where(t2, m2, jnp.where(t1, m1, lo))),
                  jnp.where(t1, jnp.where(t2, jnp.where(t3, hi, m3), m2), m1))
    gt = a >= hi
    eq = (a >= lo) & (a < hi)
    need = cap - expert_count(gt)
    sel = gt | (eq & (prefix(eq) < need))
    sel_b = as_bf16(sel)
    chunk_total = _dot(sel_b, ones)
    chunk_start = _dot(blk_before, chunk_total.astype(BF16))
    within = _dot(sel_b, before)
    o_ref[0] = jnp.where(sel, (chunk_start + within).astype(I32), -1).reshape(n_exp, n_chunk, LANES)
    cs_ref[0] = chunk_start[:, 0:1].astype(I32).reshape(n_exp, n_chunk, 1)

    rank = jnp.where(sel, within + 1.0, 0.0).astype(BF16)
    start_row = jnp.transpose(chunk_start)[0:1, :]
    end_row = jnp.transpose(chunk_start + chunk_total)[0:1, :]
    s_col = lax.broadcasted_iota(I32, (cap, rows), 0).astype(F32)
    row_id = lax.broadcasted_iota(I32, (cap, rows), 1)
    lane_f = lax.broadcasted_iota(I32, (cap, LANES), 1).astype(F32)
    for ex in range(n_exp):
        in_chunk = ((row_id // n_chunk) == ex) & (start_row <= s_col) & (s_col < end_row)
        rank_rows = _dot(jnp.where(in_chunk, 1.0, 0.0).astype(BF16), rank)
        start_s = jnp.sum(jnp.where(in_chunk, start_row, 0.0), axis=1, keepdims=True)
        chunk_s = jnp.sum(jnp.where(in_chunk, (row_id - ex * n_chunk).astype(F32), 0.0), axis=1, keepdims=True)
        match = rank_rows == (s_col[:, 0:1] + 1.0 - start_s)
        lane_s = jnp.sum(jnp.where(match, lane_f, 0.0), axis=1, keepdims=True)
        idx_ref[0, ex] = (chunk_s * LANES + lane_s).astype(I32)


def _route(aff_t, cap):
    b, e, t = aff_t.shape
    nc = t // LANES
    blk = pl.BlockSpec((1, e, nc, LANES), lambda i: (i, 0, 0, 0))
    slot, idx, chunk_start = pl.pallas_call(
        functools.partial(_route_kernel, n_exp=e, n_chunk=nc, cap=cap),
        grid=(b,),
        in_specs=[blk],
        out_specs=[blk, pl.BlockSpec((1, e, cap, 1), lambda i: (i, 0, 0, 0)),
                   pl.BlockSpec((1, e, nc, 1), lambda i: (i, 0, 0, 0))],
        out_shape=[jax.ShapeDtypeStruct((b, e, nc, LANES), I32), jax.ShapeDtypeStruct((b, e, cap, 1), I32),
                   jax.ShapeDtypeStruct((b, e, nc, 1), I32)],
        compiler_params=_params("parallel"),
        name="route_topc",
    )(aff_t.reshape(b, e, nc, LANES))
    return slot.reshape(b, e, t), idx.reshape(b, e, cap), chunk_start.reshape(b, e, nc)


SCATTER_UNROLL = 8


def _expert_kernel(idx_ref, cs_ref, hf_ref, slot_ref, aff_ref, wg_ref, wu_ref, wd_ref, acc_ref, ye_ref, xs_ref,
                   gsl_ref, *, cap, tk, win):
    t, d = hf_ref.shape[1], hf_ref.shape[2]
    nc = d // LANES

    @pl.when(pl.program_id(1) == 0)
    def _():
        acc_ref[...] = jnp.zeros_like(acc_ref)

    xs_ref[...] = jnp.zeros_like(xs_ref)
    gsl_ref[...] = jnp.zeros_like(gsl_ref)
    n_tk = t // tk
    for k in range(n_tk):
        cols = slice(k * tk, (k + 1) * tk)

        def gather(n_rows, first):
            hit = (lax.broadcasted_iota(I32, (n_rows, tk), 0) + first) == slot_ref[0, 0, :, cols]
            x = _dot(jnp.where(hit, 1.0, 0.0).astype(BF16), hf_ref[0, cols, :])
            g = jnp.sum(jnp.where(hit, aff_ref[0, 0, :, cols], 0.0), axis=1, keepdims=True)
            return x, g

        lo = cs_ref[0, 0, 0, k * (tk // LANES)]
        hi = cs_ref[0, 0, 0, (k + 1) * (tk // LANES)] if k + 1 < n_tk else cap
        first = jnp.minimum((lo // 8) * 8, cap - win)
        fits = (hi - first) <= win

        @pl.when(fits)
        def _():
            x, g = gather(win, first)
            rows = pl.ds(pl.multiple_of(first, 8), win)
            xs_ref[rows, :] += x
            gsl_ref[rows, :] += g

        @pl.when(jnp.logical_not(fits))
        def _():
            x, g = gather(cap, 0)
            xs_ref[...] += x
            gsl_ref[...] += g

    xs = xs_ref[...]
    gsl = gsl_ref[...]
    xs = xs.astype(BF16)
    gate = _dot(xs, wg_ref[0])
    up = _dot(xs, wu_ref[0])
    hid = gate / (1.0 + jnp.exp(-gate)) * up
    ye = _dot(hid.astype(BF16), wd_ref[0]) * gsl
    for c in range(nc):
        ye_ref[pl.ds(c, cap, stride=nc), :] = ye[:, c * LANES:(c + 1) * LANES]

    def tile_of(i):
        return pl.ds(pl.multiple_of(i * nc, nc), nc)

    def scatter_group(g, carry):
        base = g * SCATTER_UNROLL
        toks = [idx_ref[0, 0, 0, base + u] for u in range(SCATTER_UNROLL)]
        sums = [acc_ref[0, tile_of(toks[u]), :] + ye_ref[tile_of(base + u), :] for u in range(SCATTER_UNROLL)]
        for u in range(SCATTER_UNROLL):
            acc_ref[0, tile_of(toks[u]), :] = sums[u]
        return carry

    lax.fori_loop(0, cap // SCATTER_UNROLL, scatter_group, 0)


def _experts(hf, idx, chunk_start, slot_row, aff_row, wg, wu, wd, cap):
    b, t, d = hf.shape
    e, _, f = wg.shape
    nc = d // LANES
    tk = max(LANES, t // 4)
    win = max(8, cap // 2)
    row = pl.BlockSpec((1, 1, 1, t), lambda i, j: (i, j, 0, 0))
    smem_row = lambda n: pl.BlockSpec((1, 1, 1, n), lambda i, j: (i, j, 0, 0), memory_space=pltpu.SMEM)
    once = pl.Buffered(1)
    return pl.pallas_call(
        functools.partial(_expert_kernel, cap=cap, tk=tk, win=win),
        grid=(b, e),
        in_specs=[smem_row(cap), smem_row(t // LANES),
                  pl.BlockSpec((1, t, d), lambda i, j: (i, 0, 0), pipeline_mode=once), row, row,
                  pl.BlockSpec((1, d, f), lambda i, j: (j, 0, 0)),
                  pl.BlockSpec((1, d, f), lambda i, j: (j, 0, 0)),
                  pl.BlockSpec((1, f, d), lambda i, j: (j, 0, 0))],
        out_specs=pl.BlockSpec((1, t * nc, LANES), lambda i, j: (i, 0, 0), pipeline_mode=once),
        out_shape=jax.ShapeDtypeStruct((b, t * nc, LANES), F32),
        scratch_shapes=[pltpu.VMEM((cap * nc, LANES), F32), pltpu.VMEM((cap, d), F32), pltpu.VMEM((cap, 1), F32)],
        compiler_params=_params("parallel", "arbitrary"),
        name="expert_ffn",
    )(idx.reshape(b, e, 1, cap), chunk_start.reshape(b, e, 1, t // LANES), hf, slot_row, aff_row, wg, wu, wd)


def _combine_kernel(acc_ref, x_ref, m_ref, gn_ref, o_ref, *, final_norm):
    tt, d = o_ref.shape[1], o_ref.shape[2]
    nc = d // LANES
    moe = jnp.concatenate([acc_ref[0, pl.ds(c, tt, stride=nc), :] for c in range(nc)], axis=1)
    x2 = x_ref[0] + m_ref[0][5:6] * moe
    o_ref[0] = _rms(x2, gn_ref[...]) if final_norm else x2


def _combine(acc, x1, mod, g_final, tt, final_norm):
    b, t, d = x1.shape
    nc = d // LANES
    tile = pl.BlockSpec((1, tt, d), lambda i, j: (i, j, 0))
    return pl.pallas_call(
        functools.partial(_combine_kernel, final_norm=final_norm),
        grid=(b, t // tt),
        in_specs=[pl.BlockSpec((1, tt * nc, LANES), lambda i, j: (i, j, 0)),
                  tile, pl.BlockSpec((1, 6, d), lambda i, j: (i, 0, 0)),
                  pl.BlockSpec((1, d), lambda i, j: (0, 0))],
        out_specs=tile,
        out_shape=jax.ShapeDtypeStruct((b, t, d), F32),
        compiler_params=_params("parallel", "parallel"),
        name="moe_combine",
    )(acc, x1, mod, g_final.reshape(1, d))


def _moe(hf, aff_t, x1, mod, wg, wu, wd, g_final, final_norm):
    b, t, d = hf.shape
    e = aff_t.shape[1]
    cap = EC_CAPACITY_FACTOR * t // e
    slot, idx, chunk_start = _route(aff_t, cap)
    acc = _experts(hf, idx, chunk_start, slot.reshape(b, e, 1, t), aff_t.reshape(b, e, 1, t), wg, wu, wd, cap)
    return _combine(acc, x1, mod, g_final, min(512, t), final_norm)


def _rope(x, cos_t, sin_lo, sin_hi):
    return x * cos_t + pltpu.roll(x, LANES - QK_ROPE // 2, 1) * sin_lo + pltpu.roll(x, QK_ROPE // 2, 1) * sin_hi


def _mla_proj_kernel(x_ref, m_ref, g_ref, pos_ref, invf_ref, win_ref, qg_ref, wq_ref, kg_ref, wkn_ref, wv_ref,
                     q_ref, k_ref, v_ref):
    m = m_ref[0]
    h = _norm_mod(x_ref[0], g_ref[...], m[0:1], m[1:2]).astype(BF16)
    a = _dot(h, win_ref[...])
    cq = a[:, :Q_LORA]
    ckv = a[:, Q_LORA:Q_LORA + KV_LORA]
    kpe = a[:, Q_LORA + KV_LORA:]
    q = _dot(_rms(cq, qg_ref[...]).astype(BF16), wq_ref[...])
    kvn = _rms(ckv, kg_ref[...]).astype(BF16)
    kn = _dot(kvn, wkn_ref[...])
    v_ref[0] = _dot(kvn, wv_ref[...]).astype(BF16)
    ang = pos_ref[0] * invf_ref[...]
    lane = lax.broadcasted_iota(I32, ang.shape, 1)
    cos_t = jnp.where(lane < QK_ROPE, jnp.cos(ang), 0.0)
    sin_a = jnp.sin(ang)
    sin_lo = jnp.where(lane < QK_ROPE // 2, -sin_a, 0.0)
    sin_hi = jnp.where((lane >= QK_ROPE // 2) & (lane < QK_ROPE), sin_a, 0.0)
    kpe = _rope(kpe, cos_t, sin_lo, sin_hi)
    qs, ks = [], []
    for hd in range(MLA_HEADS):
        qs.append(q[:, hd * HEAD_PAD:hd * HEAD_PAD + QK_NOPE])
        qs.append(_rope(q[:, hd * HEAD_PAD + QK_NOPE:(hd + 1) * HEAD_PAD], cos_t, sin_lo, sin_hi))
        ks.append(kn[:, hd * QK_NOPE:(hd + 1) * QK_NOPE])
        ks.append(kpe)
    q_ref[0] = jnp.concatenate(qs, axis=1).astype(BF16)
    k_ref[0] = jnp.concatenate(ks, axis=1).astype(BF16)


def _mla_proj(x, mod, g, positions, w_in, q_g, w_qb, kv_g, w_kvb, tt):
    b, s, d = x.shape
    nh = MLA_HEADS
    pad = HEAD_PAD - QK_NOPE - QK_ROPE
    win = jnp.concatenate([w_in, jnp.zeros((d, LANES - QK_ROPE), F32)], axis=1).astype(BF16)
    wq = w_qb.reshape(Q_LORA, nh, QK_NOPE + QK_ROPE)
    wq = jnp.concatenate([wq, jnp.zeros((Q_LORA, nh, pad), F32)], axis=2).reshape(Q_LORA, nh * HEAD_PAD).astype(BF16)
    wkv = w_kvb.reshape(KV_LORA, nh, QK_NOPE + V_HEAD)
    wkn = wkv[:, :, :QK_NOPE].reshape(KV_LORA, nh * QK_NOPE).astype(BF16)
    wv = wkv[:, :, QK_NOPE:].reshape(KV_LORA, nh * V_HEAD).astype(BF16)
    inv_freq = ROPE_THETA ** (-jnp.arange(0, QK_ROPE, 2, dtype=F32) / QK_ROPE)
    invf = jnp.concatenate([inv_freq, inv_freq, jnp.zeros((LANES - QK_ROPE,), F32)]).reshape(1, LANES)
    pos = positions.astype(F32).reshape(b, s, 1)
    const = lambda a: pl.BlockSpec(a.shape, lambda i, j: (0, 0))
    tile = lambda w: pl.BlockSpec((1, tt, w), lambda i, j: (i, j, 0))
    g2, qg2, kg2 = g.reshape(1, d), q_g.reshape(1, Q_LORA), kv_g.reshape(1, KV_LORA)
    return pl.pallas_call(
        _mla_proj_kernel,
        grid=(b, s // tt),
        in_specs=[tile(d), pl.BlockSpec((1, 6, d), lambda i, j: (i, 0, 0)), const(g2), tile(1), const(invf),
                  const(win), const(qg2), const(wq), const(kg2), const(wkn), const(wv)],
        out_specs=[tile(nh * HEAD_PAD), tile(nh * HEAD_PAD), tile(nh * V_HEAD)],
        out_shape=[jax.ShapeDtypeStruct((b, s, nh * HEAD_PAD), BF16),
                   jax.ShapeDtypeStruct((b, s, nh * HEAD_PAD), BF16),
                   jax.ShapeDtypeStruct((b, s, nh * V_HEAD), BF16)],
        compiler_params=_params("parallel", "parallel"),
        name="mla_proj",
    )(x, mod, g2, pos, invf, win, qg2, wq, kg2, wkn, wv)


def _mla_attn_kernel(q_ref, k_ref, v_ref, o_ref, *, n_sub):
    sub = q_ref.shape[1] // n_sub
    for i in range(n_sub):
        s = _dot_nt(q_ref[0, i * sub:(i + 1) * sub, :], k_ref[0])
        p = jnp.exp2((s - jnp.max(s, axis=1, keepdims=True)) * (SOFTMAX_SCALE * math.log2(math.e)))
        o = _dot(p.astype(BF16), v_ref[0])
        o_ref[0, i * sub:(i + 1) * sub, :] = o / jnp.sum(p, axis=1, keepdims=True)


def _mla_attn(q, k, v, tq):
    b, s, _ = q.shape
    nh = MLA_HEADS
    return pl.pallas_call(
        functools.partial(_mla_attn_kernel, n_sub=max(1, tq // 256)),
        grid=(b, nh, s // tq),
        in_specs=[pl.BlockSpec((1, tq, HEAD_PAD), lambda i, h, j: (i, j, h)),
                  pl.BlockSpec((1, s, HEAD_PAD), lambda i, h, j: (i, 0, h)),
                  pl.BlockSpec((1, s, V_HEAD), lambda i, h, j: (i, 0, h))],
        out_specs=pl.BlockSpec((1, tq, V_HEAD), lambda i, h, j: (i, j, h)),
        out_shape=jax.ShapeDtypeStruct((b, s, nh * V_HEAD), F32),
        compiler_params=_params("parallel", "parallel", "parallel"),
        name="mla_attn",
    )(q, k, v)


def kernel(x, c, positions, ada_w, ada_b, norm_mix_g, norm_ffn_g, hy_w_in, hy_b_in, hy_conv_w, hy_conv_b, hy_f_w1, hy_f_b1, hy_f_w2, hy_f_b2, hy_f_w3, hy_f_freq, hy_f_bias, hy_w_out, hy_b_out, mla_w_in, mla_q_norm_g, mla_w_qb, mla_kv_norm_g, mla_w_kvb, mla_w_out, moe_w_router, moe_w_gate, moe_w_up, moe_w_down, final_norm_g):
    b, s, d = x.shape
    depth = ada_w.shape[0]
    assert s % LANES == 0 and b % 2 == 0 and depth == 2
    tt = min(512, s)
    mod = _ada(c, ada_w, ada_b).reshape(depth, b, 6, d)

    u_t = _hy_inproj(x, mod[0], norm_mix_g[0], hy_w_in[0].T.astype(BF16), hy_b_in[0], tt)
    cwb = jnp.concatenate([hy_conv_w[0].T, hy_conv_b[0][:, None]], axis=1)
    ct = min(16, d)
    z_g, x0c_g = _hy_conv_gate(u_t, cwb, d, min(64, d), ct)
    tabs = _dft_tables(s)
    n1, r = tabs["n1"], tabs["r"]
    k_t = _hy_filter(hy_f_w1[0], hy_f_b1[0], hy_f_w2[0], hy_f_b2[0], hy_f_w3[0], hy_f_freq[0], d, s)
    ct = min(16, d)
    kr, ki = _hy_filter_fft(k_t.reshape(d, n1, LANES), tabs, ct)
    yg = _hy_longconv(z_g, x0c_g, kr, ki, hy_f_bias[0, 0], tabs, ct)
    x1, hf, aff_t = _outproj(yg, x, mod[0], hy_w_out[0].astype(BF16), hy_b_out[0], norm_ffn_g[0],
                             moe_w_router[0].T.astype(BF16), tt, True)
    x2 = _moe(hf, aff_t, x1, mod[0], moe_w_gate[0].astype(BF16), moe_w_up[0].astype(BF16),
              moe_w_down[0].astype(BF16), final_norm_g, False)

    q, k, v = _mla_proj(x2, mod[1], norm_mix_g[1], positions, mla_w_in[0], mla_q_norm_g[0], mla_w_qb[0],
                        mla_kv_norm_g[0], mla_w_kvb[0], tt)
    o = _mla_attn(q, k, v, min(1024, s))
    x3, hf, aff_t = _outproj(o, x2, mod[1], mla_w_out[0].astype(BF16), None, norm_ffn_g[1],
                             moe_w_router[1].T.astype(BF16), tt, False)
    return _moe(hf, aff_t, x3, mod[1], moe_w_gate[1].astype(BF16), moe_w_up[1].astype(BF16),
                moe_w_down[1].astype(BF16), final_norm_g, True)
```

```python
import functools
import math

import numpy as np
import jax
import jax.numpy as jnp
from jax import lax
from jax.experimental import pallas as pl
from jax.experimental.pallas import tpu as pltpu

F32 = jnp.float32
BF16 = jnp.bfloat16
I32 = jnp.int32

EPS = 1e-6
LANES = 128
VMEM_LIMIT = 56 * 1024 * 1024

HY_BANDS = 16
HY_FAST_DECAY_PCT = 0.3
HY_SLOW_DECAY_PCT = 1.5
HY_DECAY_TARGET = 1e-2

MLA_HEADS = 8
QK_NOPE = 128
QK_ROPE = 64
V_HEAD = 128
Q_LORA = 256
KV_LORA = 128
ROPE_THETA = 10000.0
SOFTMAX_SCALE = 1.0 / math.sqrt(QK_NOPE + QK_ROPE)
HEAD_PAD = 256

EC_CAPACITY_FACTOR = 2

NT_DIMS = (((1,), (1,)), ((), ()))


def _params(*sem):
    return pltpu.CompilerParams(dimension_semantics=sem, vmem_limit_bytes=VMEM_LIMIT)


def _dot(a, b):
    return jnp.dot(a, b, preferred_element_type=F32)


def _dot_nt(a, b):
    return lax.dot_general(a, b, NT_DIMS, preferred_element_type=F32)


def _split(a):
    hi = a.astype(BF16)
    lo = (a - hi.astype(F32)).astype(BF16)
    return hi, lo


def _dot3(a, b):
    ah, al = _split(a)
    bh, bl = _split(b)
    return _dot(ah, bh) + (_dot(ah, bl) + _dot(al, bh))


def _dot1(a, b):
    return _dot(a.astype(BF16), b.astype(BF16))


def _rms(x, g):
    return x * lax.rsqrt(jnp.mean(x * x, axis=-1, keepdims=True) + EPS) * g


def _norm_mod(x, g, shift, scale):
    return _rms(x, g) * (1.0 + scale) + shift


def _ada_kernel(c_ref, w_ref, b_ref, o_ref):
    c = c_ref[...]
    cs = c / (1.0 + jnp.exp(-c))
    o_ref[0] = _dot(cs.astype(BF16), w_ref[0].astype(BF16)) + b_ref[0]


def _ada(c, ada_w, ada_b):
    depth, d, n = ada_w.shape
    b = c.shape[0]
    tn = n // 4
    return pl.pallas_call(
        _ada_kernel,
        grid=(depth, n // tn),
        in_specs=[pl.BlockSpec((b, d), lambda i, j: (0, 0)),
                  pl.BlockSpec((1, d, tn), lambda i, j: (i, 0, j)),
                  pl.BlockSpec((1, 1, tn), lambda i, j: (i, 0, j))],
        out_specs=pl.BlockSpec((1, b, tn), lambda i, j: (i, 0, j)),
        out_shape=jax.ShapeDtypeStruct((depth, b, n), F32),
        compiler_params=_params("parallel", "parallel"),
        name="ada_mod",
    )(c, ada_w, ada_b.reshape(depth, 1, n))


def _hy_inproj_kernel(x_ref, m_ref, g_ref, wt_ref, b_ref, o_ref):
    m = m_ref[0]
    h = _norm_mod(x_ref[0], g_ref[...], m[0:1], m[1:2])
    o_ref[0] = _dot_nt(wt_ref[...], h.astype(BF16)) + b_ref[...]


def _hy_inproj(x, mod, g, w_in_t, b_in, tt):
    b, s, d = x.shape
    n = w_in_t.shape[0]
    return pl.pallas_call(
        _hy_inproj_kernel,
        grid=(b, s // tt),
        in_specs=[pl.BlockSpec((1, tt, d), lambda i, j: (i, j, 0)),
                  pl.BlockSpec((1, 6, d), lambda i, j: (i, 0, 0)),
                  pl.BlockSpec((1, d), lambda i, j: (0, 0)),
                  pl.BlockSpec((n, d), lambda i, j: (0, 0)),
                  pl.BlockSpec((n, 1), lambda i, j: (0, 0))],
        out_specs=pl.BlockSpec((1, n, tt), lambda i, j: (i, 0, j)),
        out_shape=jax.ShapeDtypeStruct((b, n, s), F32),
        compiler_params=_params("parallel", "parallel"),
        name="hy_inproj",
    )(x, mod, g.reshape(1, d), w_in_t, b_in.reshape(n, 1))


def _short_conv(u, w):
    length = u.shape[1]
    lane = lax.broadcasted_iota(I32, (u.shape[0], LANES), 1)
    prev = pltpu.roll(u, 1, 1)
    nxt = pltpu.roll(u, length - 1, 1)
    prev = jnp.concatenate([jnp.where(lane == 0, 0.0, prev[:, :LANES]), prev[:, LANES:]], axis=1)
    nxt = jnp.concatenate([nxt[:, :length - LANES], jnp.where(lane == LANES - 1, 0.0, nxt[:, length - LANES:])], axis=1)
    return w[:, 0:1] * prev + w[:, 1:2] * u + w[:, 2:3] * nxt + w[:, 3:4]


def _hy_conv_gate_kernel(x0_ref, x1_ref, v_ref, w0_ref, w1_ref, wv_ref, z_ref, x0c_ref, *, group):
    ct, s = v_ref.shape[1], v_ref.shape[2]
    r = s // LANES
    z = _short_conv(v_ref[0], wv_ref[...]) * _short_conv(x1_ref[0], w1_ref[...])
    x0c = _short_conv(x0_ref[0], w0_ref[...])
    for g in range(ct // group):
        for j in range(r):
            rows = slice((g * r + j) * group, (g * r + j + 1) * group)
            chans = slice(g * group, (g + 1) * group)
            z_ref[0, rows, :] = z[chans, j * LANES:(j + 1) * LANES]
            x0c_ref[0, rows, :] = x0c[chans, j * LANES:(j + 1) * LANES]


def _hy_conv_gate(u_t, cwb, d, ct, group):
    b, _, s = u_t.shape
    nb = d // ct
    r = s // LANES
    u_spec = lambda p: pl.BlockSpec((1, ct, s), lambda i, j, p=p: (i, p * nb + j, 0))
    w_spec = lambda p: pl.BlockSpec((ct, 4), lambda i, j, p=p: (p * nb + j, 0))
    out_spec = pl.BlockSpec((1, ct * r, LANES), lambda i, j: (i, j, 0))
    return pl.pallas_call(
        functools.partial(_hy_conv_gate_kernel, group=group),
        grid=(b, nb),
        in_specs=[u_spec(0), u_spec(1), u_spec(2), w_spec(0), w_spec(1), w_spec(2)],
        out_specs=[out_spec, out_spec],
        out_shape=[jax.ShapeDtypeStruct((b, d * r, LANES), F32)] * 2,
        compiler_params=_params("parallel", "parallel"),
        name="hy_conv_gate",
    )(u_t, u_t, u_t, cwb, cwb, cwb)


def _hy_filter_kernel(w1t_ref, b1_ref, w2t_ref, b2_ref, w3t_ref, fr_ref, o_ref, *, length, lt, nfeat):
    half = pl.program_id(0)
    j = pl.program_id(1)
    d = o_ref.shape[0]
    rows = w1t_ref.shape[1]
    hi = lax.Precision.HIGHEST
    t = half * length + j * lt + lax.broadcasted_iota(I32, (rows, lt), 1)
    pos = jnp.where(half == 0, t, 2 * length - t).astype(F32)
    r = lax.broadcasted_iota(I32, (rows, lt), 0)
    tlin = pos / (length - 1.0)
    w = 2.0 * math.pi * pos / length
    fidx = jnp.where(r > HY_BANDS, r - (HY_BANDS + 1), r - 1).astype(F32)
    f = 1e-4 + fidx * ((HY_BANDS - 1 - 1e-4) / (HY_BANDS - 1))
    ang = f * w
    feat = jnp.where(r == 0, tlin, jnp.where(r <= HY_BANDS, jnp.cos(ang), -jnp.sin(ang)))
    feat = jnp.where(r < nfeat, feat, 0.0)
    fr = fr_ref[...]
    h = jnp.sin(fr * (jnp.dot(w1t_ref[...], feat, precision=hi, preferred_element_type=F32) + b1_ref[...]))
    h = jnp.sin(fr * (jnp.dot(w2t_ref[...], h, precision=hi, preferred_element_type=F32) + b2_ref[...]))
    out = jnp.dot(w3t_ref[...], h, precision=hi, preferred_element_type=F32)
    c = lax.broadcasted_iota(I32, (d, lt), 0).astype(F32)
    min_decay = math.log(HY_DECAY_TARGET) / HY_FAST_DECAY_PCT
    max_decay = math.log(HY_DECAY_TARGET) / HY_SLOW_DECAY_PCT
    delta = min_decay + c * ((max_decay - min_decay) / (d - 1.0))
    decay = jnp.exp(-tlin[0:1, :] * jnp.abs(delta))
    valid = pos[0:1, :] < length
    o_ref[...] = jnp.where(valid, out * decay, 0.0)


def _hy_filter(f_w1, f_b1, f_w2, f_b2, f_w3, f_freq, d, length):
    nfeat, hid = f_w1.shape
    rows = ((nfeat + 7) // 8) * 8
    w1t = jnp.zeros((hid, rows), F32).at[:, :nfeat].set(f_w1.T)
    lt = min(512, length)
    col = lambda a: a.reshape(hid, 1)
    small = lambda shape: pl.BlockSpec(shape, lambda h, j: (0, 0))
    return pl.pallas_call(
        functools.partial(_hy_filter_kernel, length=length, lt=lt, nfeat=nfeat),
        grid=(2, length // lt),
        in_specs=[small((hid, rows)), small((hid, 1)), small((hid, hid)), small((hid, 1)),
                  pl.BlockSpec((d, hid), lambda h, j: (h, 0)), small((hid, 1))],
        out_specs=pl.BlockSpec((d, lt), lambda h, j: (0, h * (length // lt) + j)),
        out_shape=jax.ShapeDtypeStruct((d, 2 * length), F32),
        compiler_params=_params("parallel", "parallel"),
        name="hy_filter",
    )(w1t, col(f_b1), f_w2.T, col(f_b2), f_w3.T, col(f_freq))


def _dft_tables(length):
    n = 2 * length
    n1 = n // LANES
    r = n1 // 2
    k1 = np.arange(n1)[:, None]
    ang1 = 2.0 * np.pi * k1 * np.arange(n1)[None, :] / n1
    c1, s1 = np.cos(ang1), np.sin(ang1)
    m1_real = np.concatenate([c1, -s1], axis=0)
    m1_cplx = np.block([[c1[:, :r], s1[:, :r]], [-s1[:, :r], c1[:, :r]]])
    m1_inv = np.block([[c1[:r, :], -s1[:r, :]], [s1[:r, :], c1[:r, :]]])
    ang_t = 2.0 * np.pi * k1 * np.arange(LANES)[None, :] / n
    tw = np.stack([np.cos(ang_t), np.sin(ang_t)])
    n2 = np.arange(LANES)
    ang2 = 2.0 * np.pi * n2[:, None] * n2[None, :] / LANES
    c2, s2 = np.cos(ang2), np.sin(ang2)
    w2_fwd = np.block([[c2, -s2], [s2, c2]])
    w2_inv = np.block([[c2, s2], [-s2, c2]])
    f = lambda a: jnp.asarray(a, F32)
    return dict(n1=n1, r=r, m1_real=f(m1_real), m1_cplx=f(m1_cplx), m1_inv=f(m1_inv),
                tw=f(tw), w2_fwd=f(w2_fwd), w2_inv=f(w2_inv))


def _wide(ref3, ct):
    return jnp.concatenate([ref3[c] for c in range(ct)], axis=1)


def _fft_fwd(x_wide, m1, tc, ts, w2, ct, n1, mm):
    a = mm(m1, x_wide)
    ar, ai = a[:n1], a[n1:]
    tall = []
    for c in range(ct):
        arc, aic = ar[:, c * LANES:(c + 1) * LANES], ai[:, c * LANES:(c + 1) * LANES]
        tall.append(jnp.concatenate([arc * tc + aic * ts, aic * tc - arc * ts], axis=1))
    return mm(jnp.concatenate(tall, axis=0), w2)


def _fft_inv(y_tall, w2i, tc, ts, m1i, ct, n1, mm):
    cc = mm(y_tall, w2i)
    dr, di = [], []
    for c in range(ct):
        cr, ci = cc[c * n1:(c + 1) * n1, :LANES], cc[c * n1:(c + 1) * n1, LANES:]
        dr.append(cr * tc - ci * ts)
        di.append(ci * tc + cr * ts)
    d_wide = jnp.concatenate([jnp.concatenate(dr, axis=1), jnp.concatenate(di, axis=1)], axis=0)
    return mm(m1i, d_wide)


def _hy_filter_fft_kernel(k_ref, m1_ref, tw_ref, w2_ref, kr_ref, ki_ref, *, ct, n1):
    x = _fft_fwd(_wide(k_ref, ct), m1_ref[...], tw_ref[0], tw_ref[1], w2_ref[...], ct, n1, _dot3)
    x = x * (1.0 / (n1 * LANES))
    kr_ref[...] = x[:, :LANES].reshape(ct, n1, LANES)
    ki_ref[...] = x[:, LANES:].reshape(ct, n1, LANES)


def _hy_filter_fft(k4, tabs, ct):
    d, n1, _ = k4.shape
    const = lambda a: pl.BlockSpec(a.shape, lambda i, nd=a.ndim: (0,) * nd)
    blk = pl.BlockSpec((ct, n1, LANES), lambda i: (i, 0, 0))
    return pl.pallas_call(
        functools.partial(_hy_filter_fft_kernel, ct=ct, n1=n1),
        grid=(d // ct,),
        in_specs=[blk, const(tabs["m1_real"]), const(tabs["tw"]), const(tabs["w2_fwd"])],
        out_specs=[blk, blk],
        out_shape=[jax.ShapeDtypeStruct((d, n1, LANES), F32)] * 2,
        compiler_params=_params("parallel"),
        name="hy_filter_fft",
    )(k4, tabs["m1_real"], tabs["tw"], tabs["w2_fwd"])


def _hy_longconv_kernel(z_ref, x0_ref, kr_ref, ki_ref, bias_ref, m1_ref, m1i_ref, tw_ref, w2_ref, w2i_ref,
                        o_ref, y_ref, *, ct, n1, r):
    tc, ts = tw_ref[0], tw_ref[1]
    def chan(ref, p, c):
        return ref[p, pl.ds(c, r, stride=ct), :]

    x_wide = jnp.concatenate([jnp.concatenate([chan(z_ref, p, c) for c in range(ct)], axis=1) for p in range(2)],
                             axis=0)
    x = _fft_fwd(x_wide, m1_ref[...], tc, ts, w2_ref[...], ct, n1, _dot1)
    xr, xi = x[:, :LANES], x[:, LANES:]
    kr = kr_ref[...].reshape(ct * n1, LANES)
    ki = ki_ref[...].reshape(ct * n1, LANES)
    y_tall = jnp.concatenate([xr * kr - xi * ki, xr * ki + xi * kr], axis=1)
    y = _fft_inv(y_tall, w2i_ref[...], tc, ts, m1i_ref[...], ct, n1, _dot1)
    for p in range(2):
        for c in range(ct):
            conv = y[p * r:(p + 1) * r, c * LANES:(c + 1) * LANES]
            y_ref[p, c * r:(c + 1) * r, :] = (conv + bias_ref[c] * chan(z_ref, p, c)) * chan(x0_ref, p, c)
        for j in range(r):
            o_ref[p, :, j * LANES:(j + 1) * LANES] = y_ref[p, pl.ds(j, ct, stride=r), :]


def _hy_longconv(z, x0c, kr, ki, bias, tabs, ct):
    b, d = z.shape[0], bias.shape[0]
    n1, r = tabs["n1"], tabs["r"]
    const = lambda a: pl.BlockSpec(a.shape, lambda i, j, nd=a.ndim: (0,) * nd)
    data = pl.BlockSpec((2, ct * r, LANES), lambda i, j: (j, i, 0))
    spec = pl.BlockSpec((ct, n1, LANES), lambda i, j: (i, 0, 0))
    consts = [tabs["m1_cplx"], tabs["m1_inv"], tabs["tw"], tabs["w2_fwd"], tabs["w2_inv"]]
    return pl.pallas_call(
        functools.partial(_hy_longconv_kernel, ct=ct, n1=n1, r=r),
        grid=(d // ct, b // 2),
        in_specs=[data, data, spec, spec, pl.BlockSpec((ct, 1, 1), lambda i, j: (i, 0, 0))]
                 + [const(a) for a in consts],
        out_specs=pl.BlockSpec((2, ct, r * LANES), lambda i, j: (j, i, 0)),
        out_shape=jax.ShapeDtypeStruct((b, d, r * LANES), F32),
        scratch_shapes=[pltpu.VMEM((2, ct * r, LANES), F32)],
        compiler_params=_params("parallel", "parallel"),
        name="hy_longconv",
    )(z, x0c, kr, ki, bias.reshape(d, 1, 1), *consts)


def _router_tail(x1, m, gf_ref, wrt_ref, x1_ref, hf_ref, aff_ref):
    x1_ref[0] = x1
    hf = _norm_mod(x1, gf_ref[...], m[3:4], m[4:5]).astype(BF16)
    hf_ref[0] = hf
    logits = _dot_nt(wrt_ref[...], hf)
    ex = jnp.exp(logits - jnp.max(logits, axis=0, keepdims=True))
    aff_ref[0] = ex / jnp.sum(ex, axis=0, keepdims=True)


def _hy_outproj_kernel(y_ref, x_ref, m_ref, w_ref, b_ref, gf_ref, wrt_ref, x1_ref, hf_ref, aff_ref):
    m = m_ref[0]
    y = jnp.transpose(y_ref[0]).astype(BF16)
    x1 = x_ref[0] + m[2:3] * (_dot(y, w_ref[...]) + b_ref[...])
    _router_tail(x1, m, gf_ref, wrt_ref, x1_ref, hf_ref, aff_ref)


def _mla_outproj_kernel(y_ref, x_ref, m_ref, w_ref, gf_ref, wrt_ref, x1_ref, hf_ref, aff_ref):
    m = m_ref[0]
    x1 = x_ref[0] + m[2:3] * _dot(y_ref[0].astype(BF16), w_ref[...])
    _router_tail(x1, m, gf_ref, wrt_ref, x1_ref, hf_ref, aff_ref)


def _outproj(y, x, mod, w_out, b_out, g_ffn, w_router_t, tt, channel_major):
    b, s, d = x.shape
    e = w_router_t.shape[0]
    dy = w_out.shape[0]
    tile = pl.BlockSpec((1, tt, d), lambda i, j: (i, j, 0))
    const = lambda shape: pl.BlockSpec(shape, lambda i, j: (0, 0))
    if channel_major:
        y_spec = pl.BlockSpec((1, dy, tt), lambda i, j: (i, 0, j))
        kern, extra, extra_specs = _hy_outproj_kernel, [b_out.reshape(1, d)], [const((1, d))]
    else:
        y_spec = pl.BlockSpec((1, tt, dy), lambda i, j: (i, j, 0))
        kern, extra, extra_specs = _mla_outproj_kernel, [], []
    return pl.pallas_call(
        kern,
        grid=(b, s // tt),
        in_specs=[y_spec, tile, pl.BlockSpec((1, 6, d), lambda i, j: (i, 0, 0)), const((dy, d))]
                 + extra_specs + [const((1, d)), const((e, d))],
        out_specs=[tile, tile, pl.BlockSpec((1, e, tt), lambda i, j: (i, 0, j))],
        out_shape=[jax.ShapeDtypeStruct((b, s, d), F32), jax.ShapeDtypeStruct((b, s, d), BF16),
                   jax.ShapeDtypeStruct((b, e, s), F32)],
        compiler_params=_params("parallel", "parallel"),
        name="outproj_router",
    )(y, x, mod, w_out, *extra, g_ffn.reshape(1, d), w_router_t)


def _route_kernel(a_ref, o_ref, idx_ref, cs_ref, *, n_exp, n_chunk, cap):
    rows = n_exp * n_chunk
    a = a_ref[0].reshape(rows, LANES)
    ri = lax.broadcasted_iota(I32, (rows, rows), 0)
    rj = lax.broadcasted_iota(I32, (rows, rows), 1)
    same = (ri // n_chunk) == (rj // n_chunk)
    blk_all = jnp.where(same, 1.0, 0.0).astype(BF16)
    blk_before = jnp.where(same & (rj < ri), 1.0, 0.0).astype(BF16)
    li = lax.broadcasted_iota(I32, (LANES, LANES), 0)
    lj = lax.broadcasted_iota(I32, (LANES, LANES), 1)
    ones = jnp.ones((LANES, LANES), BF16)
    before = jnp.where(li < lj, 1.0, 0.0).astype(BF16)

    def as_bf16(mask):
        return jnp.where(mask, 1.0, 0.0).astype(BF16)

    def expert_count(mask):
        return _dot(blk_all, _dot(as_bf16(mask), ones).astype(BF16))

    def prefix(mask):
        mb = as_bf16(mask)
        return _dot(mb, before) + _dot(blk_before, _dot(mb, ones).astype(BF16))

    def enough(threshold):
        return expert_count(a >= threshold) >= cap

    cur = jnp.zeros((rows, LANES), I32)
    for bit in range(30, 0, -2):
        b1, b0 = 1 << bit, 1 << (bit - 1)
        c1, c0, c10 = cur | b1, cur | b0, cur | (b1 | b0)
        t1, t0, t10 = (enough(pltpu.bitcast(c, F32)) for c in (c1, c0, c10))
        cur = jnp.where(t10, c10, jnp.where(t1, c1, jnp.where(t0, c0, cur)))
    cur = jnp.where(enough(pltpu.bitcast(cur | 1, F32)), cur | 1, cur)
    lo = pltpu.bitcast(cur, F32)
    hi = pltpu.bitcast(cur + 1, F32)
    for _ in range(2):
        width = hi - lo
        m1, m2, m3 = lo + width * 0.25, lo + width * 0.5, lo + width * 0.75
        t1, t2, t3 = enough(m1), enough(m2), enough(m3)
        lo, hi = (jnp.where(t3, m3, jnp.where(t2, m2, jnp.where(t1, m1, lo))),
                  jnp.where(t1, jnp.where(t2, jnp.where(t3, hi, m3), m2), m1))
    gt = a >= hi
    eq = (a >= lo) & (a < hi)
    need = cap - expert_count(gt)
    sel = gt | (eq & (prefix(eq) < need))
    sel_b = as_bf16(sel)
    chunk_total = _dot(sel_b, ones)
    chunk_start = _dot(blk_before, chunk_total.astype(BF16))
    within = _dot(sel_b, before)
    o_ref[0] = jnp.where(sel, (chunk_start + within).astype(I32), -1).reshape(n_exp, n_chunk, LANES)
    cs_ref[0] = chunk_start[:, 0:1].astype(I32).reshape(n_exp, n_chunk, 1)

    rank = jnp.where(sel, within + 1.0, 0.0).astype(BF16)
    start_row = jnp.transpose(chunk_start)[0:1, :]
    end_row = jnp.transpose(chunk_start + chunk_total)[0:1, :]
    s_col = lax.broadcasted_iota(I32, (cap, rows), 0).astype(F32)
    row_id = lax.broadcasted_iota(I32, (cap, rows), 1)
    lane_f = lax.broadcasted_iota(I32, (cap, LANES), 1).astype(F32)
    for ex in range(n_exp):
        in_chunk = ((row_id // n_chunk) == ex) & (start_row <= s_col) & (s_col < end_row)
        rank_rows = _dot(jnp.where(in_chunk, 1.0, 0.0).astype(BF16), rank)
        start_s = jnp.sum(jnp.where(in_chunk, start_row, 0.0), axis=1, keepdims=True)
        chunk_s = jnp.sum(jnp.where(in_chunk, (row_id - ex * n_chunk).astype(F32), 0.0), axis=1, keepdims=True)
        match = rank_rows == (s_col[:, 0:1] + 1.0 - start_s)
        lane_s = jnp.sum(jnp.where(match, lane_f, 0.0), axis=1, keepdims=True)
        idx_ref[0, ex] = (chunk_s * LANES + lane_s).astype(I32)


def _route(aff_t, cap):
    b, e, t = aff_t.shape
    nc = t // LANES
    blk = pl.BlockSpec((1, e, nc, LANES), lambda i: (i, 0, 0, 0))
    slot, idx, chunk_start = pl.pallas_call(
        functools.partial(_route_kernel, n_exp=e, n_chunk=nc, cap=cap),
        grid=(b,),
        in_specs=[blk],
        out_specs=[blk, pl.BlockSpec((1, e, cap, 1), lambda i: (i, 0, 0, 0)),
                   pl.BlockSpec((1, e, nc, 1), lambda i: (i, 0, 0, 0))],
        out_shape=[jax.ShapeDtypeStruct((b, e, nc, LANES), I32), jax.ShapeDtypeStruct((b, e, cap, 1), I32),
                   jax.ShapeDtypeStruct((b, e, nc, 1), I32)],
        compiler_params=_params("parallel"),
        name="route_topc",
    )(aff_t.reshape(b, e, nc, LANES))
    return slot.reshape(b, e, t), idx.reshape(b, e, cap), chunk_start.reshape(b, e, nc)


SCATTER_UNROLL = 8


def _expert_kernel(idx_ref, cs_ref, hf_ref, slot_ref, aff_ref, wg_ref, wu_ref, wd_ref, acc_ref, ye_ref, xs_ref,
                   gsl_ref, *, cap, tk, win):
    t, d = hf_ref.shape[1], hf_ref.shape[2]
    nc = d // LANES

    @pl.when(pl.program_id(1) == 0)
    def _():
        acc_ref[...] = jnp.zeros_like(acc_ref)

    xs_ref[...] = jnp.zeros_like(xs_ref)
    gsl_ref[...] = jnp.zeros_like(gsl_ref)
    n_tk = t // tk
    for k in range(n_tk):
        cols = slice(k * tk, (k + 1) * tk)

        def gather(n_rows, first):
            hit = (lax.broadcasted_iota(I32, (n_rows, tk), 0) + first) == slot_ref[0, 0, :, cols]
            x = _dot(jnp.where(hit, 1.0, 0.0).astype(BF16), hf_ref[0, cols, :])
            g = jnp.sum(jnp.where(hit, aff_ref[0, 0, :, cols], 0.0), axis=1, keepdims=True)
            return x, g

        lo = cs_ref[0, 0, 0, k * (tk // LANES)]
        hi = cs_ref[0, 0, 0, (k + 1) * (tk // LANES)] if k + 1 < n_tk else cap
        first = jnp.minimum((lo // 8) * 8, cap - win)
        fits = (hi - first) <= win

        @pl.when(fits)
        def _():
            x, g = gather(win, first)
            rows = pl.ds(pl.multiple_of(first, 8), win)
            xs_ref[rows, :] += x
            gsl_ref[rows, :] += g

        @pl.when(jnp.logical_not(fits))
        def _():
            x, g = gather(cap, 0)
            xs_ref[...] += x
            gsl_ref[...] += g

    xs = xs_ref[...]
    gsl = gsl_ref[...]
    xs = xs.astype(BF16)
    gate = _dot(xs, wg_ref[0])
    up = _dot(xs, wu_ref[0])
    hid = gate / (1.0 + jnp.exp(-gate)) * up
    ye = _dot(hid.astype(BF16), wd_ref[0]) * gsl
    for c in range(nc):
        ye_ref[pl.ds(c, cap, stride=nc), :] = ye[:, c * LANES:(c + 1) * LANES]

    def tile_of(i):
        return pl.ds(pl.multiple_of(i * nc, nc), nc)

    def scatter_group(g, carry):
        base = g * SCATTER_UNROLL
        toks = [idx_ref[0, 0, 0, base + u] for u in range(SCATTER_UNROLL)]
        sums = [acc_ref[0, tile_of(toks[u]), :] + ye_ref[tile_of(base + u), :] for u in range(SCATTER_UNROLL)]
        for u in range(SCATTER_UNROLL):
            acc_ref[0, tile_of(toks[u]), :] = sums[u]
        return carry

    lax.fori_loop(0, cap // SCATTER_UNROLL, scatter_group, 0)


def _experts(hf, idx, chunk_start, slot_row, aff_row, wg, wu, wd, cap):
    b, t, d = hf.shape
    e, _, f = wg.shape
    nc = d // LANES
    tk = max(LANES, t // 4)
    win = max(8, cap // 2)
    row = pl.BlockSpec((1, 1, 1, t), lambda i, j: (i, j, 0, 0))
    smem_row = lambda n: pl.BlockSpec((1, 1, 1, n), lambda i, j: (i, j, 0, 0), memory_space=pltpu.SMEM)
    once = pl.Buffered(1)
    return pl.pallas_call(
        functools.partial(_expert_kernel, cap=cap, tk=tk, win=win),
        grid=(b, e),
        in_specs=[smem_row(cap), smem_row(t // LANES),
                  pl.BlockSpec((1, t, d), lambda i, j: (i, 0, 0), pipeline_mode=once), row, row,
                  pl.BlockSpec((1, d, f), lambda i, j: (j, 0, 0)),
                  pl.BlockSpec((1, d, f), lambda i, j: (j, 0, 0)),
                  pl.BlockSpec((1, f, d), lambda i, j: (j, 0, 0))],
        out_specs=pl.BlockSpec((1, t * nc, LANES), lambda i, j: (i, 0, 0), pipeline_mode=once),
        out_shape=jax.ShapeDtypeStruct((b, t * nc, LANES), F32),
        scratch_shapes=[pltpu.VMEM((cap * nc, LANES), F32), pltpu.VMEM((cap, d), F32), pltpu.VMEM((cap, 1), F32)],
        compiler_params=_params("parallel", "arbitrary"),
        name="expert_ffn",
    )(idx.reshape(b, e, 1, cap), chunk_start.reshape(b, e, 1, t // LANES), hf, slot_row, aff_row, wg, wu, wd)


def _combine_kernel(acc_ref, x_ref, m_ref, gn_ref, o_ref, *, final_norm):
    tt, d = o_ref.shape[1], o_ref.shape[2]
    nc = d // LANES
    moe = jnp.concatenate([acc_ref[0, pl.ds(c, tt, stride=nc), :] for c in range(nc)], axis=1)
    x2 = x_ref[0] + m_ref[0][5:6] * moe
    o_ref[0] = _rms(x2, gn_ref[...]) if final_norm else x2


def _combine(acc, x1, mod, g_final, tt, final_norm):
    b, t, d = x1.shape
    nc = d // LANES
    tile = pl.BlockSpec((1, tt, d), lambda i, j: (i, j, 0))
    return pl.pallas_call(
        functools.partial(_combine_kernel, final_norm=final_norm),
        grid=(b, t // tt),
        in_specs=[pl.BlockSpec((1, tt * nc, LANES), lambda i, j: (i, j, 0)),
                  tile, pl.BlockSpec((1, 6, d), lambda i, j: (i, 0, 0)),
                  pl.BlockSpec((1, d), lambda i, j: (0, 0))],
        out_specs=tile,
        out_shape=jax.ShapeDtypeStruct((b, t, d), F32),
        compiler_params=_params("parallel", "parallel"),
        name="moe_combine",
    )(acc, x1, mod, g_final.reshape(1, d))


def _moe(hf, aff_t, x1, mod, wg, wu, wd, g_final, final_norm):
    b, t, d = hf.shape
    e = aff_t.shape[1]
    cap = EC_CAPACITY_FACTOR * t // e
    slot, idx, chunk_start = _route(aff_t, cap)
    acc = _experts(hf, idx, chunk_start, slot.reshape(b, e, 1, t), aff_t.reshape(b, e, 1, t), wg, wu, wd, cap)
    return _combine(acc, x1, mod, g_final, min(512, t), final_norm)


def _rope(x, cos_t, sin_lo, sin_hi):
    return x * cos_t + pltpu.roll(x, LANES - QK_ROPE // 2, 1) * sin_lo + pltpu.roll(x, QK_ROPE // 2, 1) * sin_hi


def _mla_proj_kernel(x_ref, m_ref, g_ref, pos_ref, invf_ref, win_ref, qg_ref, wq_ref, kg_ref, wkn_ref, wv_ref,
                     q_ref, k_ref, v_ref):
    m = m_ref[0]
    h = _norm_mod(x_ref[0], g_ref[...], m[0:1], m[1:2]).astype(BF16)
    a = _dot(h, win_ref[...])
    cq = a[:, :Q_LORA]
    ckv = a[:, Q_LORA:Q_LORA + KV_LORA]
    kpe = a[:, Q_LORA + KV_LORA:]
    q = _dot(_rms(cq, qg_ref[...]).astype(BF16), wq_ref[...])
    kvn = _rms(ckv, kg_ref[...]).astype(BF16)
    kn = _dot(kvn, wkn_ref[...])
    v_ref[0] = _dot(kvn, wv_ref[...]).astype(BF16)
    ang = pos_ref[0] * invf_ref[...]
    lane = lax.broadcasted_iota(I32, ang.shape, 1)
    cos_t = jnp.where(lane < QK_ROPE, jnp.cos(ang), 0.0)
    sin_a = jnp.sin(ang)
    sin_lo = jnp.where(lane < QK_ROPE // 2, -sin_a, 0.0)
    sin_hi = jnp.where((lane >= QK_ROPE // 2) & (lane < QK_ROPE), sin_a, 0.0)
    kpe = _rope(kpe, cos_t, sin_lo, sin_hi)
    qs, ks = [], []
    for hd in range(MLA_HEADS):
        qs.append(q[:, hd * HEAD_PAD:hd * HEAD_PAD + QK_NOPE])
        qs.append(_rope(q[:, hd * HEAD_PAD + QK_NOPE:(hd + 1) * HEAD_PAD], cos_t, sin_lo, sin_hi))
        ks.append(kn[:, hd * QK_NOPE:(hd + 1) * QK_NOPE])
        ks.append(kpe)
    q_ref[0] = jnp.concatenate(qs, axis=1).astype(BF16)
    k_ref[0] = jnp.concatenate(ks, axis=1).astype(BF16)


def _mla_proj(x, mod, g, positions, w_in, q_g, w_qb, kv_g, w_kvb, tt):
    b, s, d = x.shape
    nh = MLA_HEADS
    pad = HEAD_PAD - QK_NOPE - QK_ROPE
    win = jnp.concatenate([w_in, jnp.zeros((d, LANES - QK_ROPE), F32)], axis=1).astype(BF16)
    wq = w_qb.reshape(Q_LORA, nh, QK_NOPE + QK_ROPE)
    wq = jnp.concatenate([wq, jnp.zeros((Q_LORA, nh, pad), F32)], axis=2).reshape(Q_LORA, nh * HEAD_PAD).astype(BF16)
    wkv = w_kvb.reshape(KV_LORA, nh, QK_NOPE + V_HEAD)
    wkn = wkv[:, :, :QK_NOPE].reshape(KV_LORA, nh * QK_NOPE).astype(BF16)
    wv = wkv[:, :, QK_NOPE:].reshape(KV_LORA, nh * V_HEAD).astype(BF16)
    inv_freq = ROPE_THETA ** (-jnp.arange(0, QK_ROPE, 2, dtype=F32) / QK_ROPE)
    invf = jnp.concatenate([inv_freq, inv_freq, jnp.zeros((LANES - QK_ROPE,), F32)]).reshape(1, LANES)
    pos = positions.astype(F32).reshape(b, s, 1)
    const = lambda a: pl.BlockSpec(a.shape, lambda i, j: (0, 0))
    tile = lambda w: pl.BlockSpec((1, tt, w), lambda i, j: (i, j, 0))
    g2, qg2, kg2 = g.reshape(1, d), q_g.reshape(1, Q_LORA), kv_g.reshape(1, KV_LORA)
    return pl.pallas_call(
        _mla_proj_kernel,
        grid=(b, s // tt),
        in_specs=[tile(d), pl.BlockSpec((1, 6, d), lambda i, j: (i, 0, 0)), const(g2), tile(1), const(invf),
                  const(win), const(qg2), const(wq), const(kg2), const(wkn), const(wv)],
        out_specs=[tile(nh * HEAD_PAD), tile(nh * HEAD_PAD), tile(nh * V_HEAD)],
        out_shape=[jax.ShapeDtypeStruct((b, s, nh * HEAD_PAD), BF16),
                   jax.ShapeDtypeStruct((b, s, nh * HEAD_PAD), BF16),
                   jax.ShapeDtypeStruct((b, s, nh * V_HEAD), BF16)],
        compiler_params=_params("parallel", "parallel"),
        name="mla_proj",
    )(x, mod, g2, pos, invf, win, qg2, wq, kg2, wkn, wv)


def _mla_attn_kernel(q_ref, k_ref, v_ref, o_ref, *, n_sub):
    sub = q_ref.shape[1] // n_sub
    for i in range(n_sub):
        s = _dot_nt(q_ref[0, i * sub:(i + 1) * sub, :], k_ref[0])
        p = jnp.exp2((s - jnp.max(s, axis=1, keepdims=True)) * (SOFTMAX_SCALE * math.log2(math.e)))
        o = _dot(p.astype(BF16), v_ref[0])
        o_ref[0, i * sub:(i + 1) * sub, :] = o / jnp.sum(p, axis=1, keepdims=True)


def _mla_attn(q, k, v, tq):
    b, s, _ = q.shape
    nh = MLA_HEADS
    return pl.pallas_call(
        functools.partial(_mla_attn_kernel, n_sub=max(1, tq // 256)),
        grid=(b, nh, s // tq),
        in_specs=[pl.BlockSpec((1, tq, HEAD_PAD), lambda i, h, j: (i, j, h)),
                  pl.BlockSpec((1, s, HEAD_PAD), lambda i, h, j: (i, 0, h)),
                  pl.BlockSpec((1, s, V_HEAD), lambda i, h, j: (i, 0, h))],
        out_specs=pl.BlockSpec((1, tq, V_HEAD), lambda i, h, j: (i, j, h)),
        out_shape=jax.ShapeDtypeStruct((b, s, nh * V_HEAD), F32),
        compiler_params=_params("parallel", "parallel", "parallel"),
        name="mla_attn",
    )(q, k, v)


def kernel(x, c, positions, ada_w, ada_b, norm_mix_g, norm_ffn_g, hy_w_in, hy_b_in, hy_conv_w, hy_conv_b, hy_f_w1, hy_f_b1, hy_f_w2, hy_f_b2, hy_f_w3, hy_f_freq, hy_f_bias, hy_w_out, hy_b_out, mla_w_in, mla_q_norm_g, mla_w_qb, mla_kv_norm_g, mla_w_kvb, mla_w_out, moe_w_router, moe_w_gate, moe_w_up, moe_w_down, final_norm_g):
    b, s, d = x.shape
    depth = ada_w.shape[0]
    assert s % LANES == 0 and b % 2 == 0 and depth == 2
    tt = min(512, s)
    mod = _ada(c, ada_w, ada_b).reshape(depth, b, 6, d)

    u_t = _hy_inproj(x, mod[0], norm_mix_g[0], hy_w_in[0].T.astype(BF16), hy_b_in[0], tt)
    cwb = jnp.concatenate([hy_conv_w[0].T, hy_conv_b[0][:, None]], axis=1)
    ct = min(16, d)
    z_g, x0c_g = _hy_conv_gate(u_t, cwb, d, min(64, d), ct)
    tabs = _dft_tables(s)
    n1, r = tabs["n1"], tabs["r"]
    k_t = _hy_filter(hy_f_w1[0], hy_f_b1[0], hy_f_w2[0], hy_f_b2[0], hy_f_w3[0], hy_f_freq[0], d, s)
    ct = min(16, d)
    kr, ki = _hy_filter_fft(k_t.reshape(d, n1, LANES), tabs, ct)
    yg = _hy_longconv(z_g, x0c_g, kr, ki, hy_f_bias[0, 0], tabs, ct)
    x1, hf, aff_t = _outproj(yg, x, mod[0], hy_w_out[0].astype(BF16), hy_b_out[0], norm_ffn_g[0],
                             moe_w_router[0].T.astype(BF16), tt, True)
    x2 = _moe(hf, aff_t, x1, mod[0], moe_w_gate[0].astype(BF16), moe_w_up[0].astype(BF16),
              moe_w_down[0].astype(BF16), final_norm_g, False)

    q, k, v = _mla_proj(x2, mod[1], norm_mix_g[1], positions, mla_w_in[0], mla_q_norm_g[0], mla_w_qb[0],
                        mla_kv_norm_g[0], mla_w_kvb[0], tt)
    o = _mla_attn(q, k, v, min(2048, s))
    x3, hf, aff_t = _outproj(o, x2, mod[1], mla_w_out[0].astype(BF16), None, norm_ffn_g[1],
                             moe_w_router[1].T.astype(BF16), tt, False)
    return _moe(hf, aff_t, x3, mod[1], moe_w_gate[1].astype(BF16), moe_w_up[1].astype(BF16),
                moe_w_down[1].astype(BF16), final_norm_g, True)
```

```python
import functools
import math

import numpy as np
import jax
import jax.numpy as jnp
from jax import lax
from jax.experimental import pallas as pl
from jax.experimental.pallas import tpu as pltpu

F32 = jnp.float32
BF16 = jnp.bfloat16
I32 = jnp.int32

EPS = 1e-6
LANES = 128
VMEM_LIMIT = 56 * 1024 * 1024

HY_BANDS = 16
HY_FAST_DECAY_PCT = 0.3
HY_SLOW_DECAY_PCT = 1.5
HY_DECAY_TARGET = 1e-2

MLA_HEADS = 8
QK_NOPE = 128
QK_ROPE = 64
V_HEAD = 128
Q_LORA = 256
KV_LORA = 128
ROPE_THETA = 10000.0
SOFTMAX_SCALE = 1.0 / math.sqrt(QK_NOPE + QK_ROPE)
HEAD_PAD = 256

EC_CAPACITY_FACTOR = 2

NT_DIMS = (((1,), (1,)), ((), ()))


def _params(*sem):
    return pltpu.CompilerParams(dimension_semantics=sem, vmem_limit_bytes=VMEM_LIMIT)


def _dot(a, b):
    return jnp.dot(a, b, preferred_element_type=F32)


def _dot_nt(a, b):
    return lax.dot_general(a, b, NT_DIMS, preferred_element_type=F32)


def _split(a):
    hi = a.astype(BF16)
    lo = (a - hi.astype(F32)).astype(BF16)
    return hi, lo


def _dot3(a, b):
    ah, al = _split(a)
    bh, bl = _split(b)
    return _dot(ah, bh) + (_dot(ah, bl) + _dot(al, bh))


def _dot1(a, b):
    return _dot(a.astype(BF16), b.astype(BF16))


def _rms(x, g):
    return x * lax.rsqrt(jnp.mean(x * x, axis=-1, keepdims=True) + EPS) * g


def _norm_mod(x, g, shift, scale):
    return _rms(x, g) * (1.0 + scale) + shift


def _ada_kernel(c_ref, w_ref, b_ref, o_ref):
    c = c_ref[...]
    cs = c / (1.0 + jnp.exp(-c))
    o_ref[0] = _dot(cs.astype(BF16), w_ref[0].astype(BF16)) + b_ref[0]


def _ada(c, ada_w, ada_b):
    depth, d, n = ada_w.shape
    b = c.shape[0]
    tn = n // 4
    return pl.pallas_call(
        _ada_kernel,
        grid=(depth, n // tn),
        in_specs=[pl.BlockSpec((b, d), lambda i, j: (0, 0)),
                  pl.BlockSpec((1, d, tn), lambda i, j: (i, 0, j)),
                  pl.BlockSpec((1, 1, tn), lambda i, j: (i, 0, j))],
        out_specs=pl.BlockSpec((1, b, tn), lambda i, j: (i, 0, j)),
        out_shape=jax.ShapeDtypeStruct((depth, b, n), F32),
        compiler_params=_params("parallel", "parallel"),
        name="ada_mod",
    )(c, ada_w, ada_b.reshape(depth, 1, n))


def _hy_inproj_kernel(x_ref, m_ref, g_ref, wt_ref, b_ref, o_ref):
    m = m_ref[0]
    h = _norm_mod(x_ref[0], g_ref[...], m[0:1], m[1:2])
    o_ref[0] = _dot_nt(wt_ref[...], h.astype(BF16)) + b_ref[...]


def _hy_inproj(x, mod, g, w_in_t, b_in, tt):
    b, s, d = x.shape
    n = w_in_t.shape[0]
    return pl.pallas_call(
        _hy_inproj_kernel,
        grid=(b, s // tt),
        in_specs=[pl.BlockSpec((1, tt, d), lambda i, j: (i, j, 0)),
                  pl.BlockSpec((1, 6, d), lambda i, j: (i, 0, 0)),
                  pl.BlockSpec((1, d), lambda i, j: (0, 0)),
                  pl.BlockSpec((n, d), lambda i, j: (0, 0)),
                  pl.BlockSpec((n, 1), lambda i, j: (0, 0))],
        out_specs=pl.BlockSpec((1, n, tt), lambda i, j: (i, 0, j)),
        out_shape=jax.ShapeDtypeStruct((b, n, s), F32),
        compiler_params=_params("parallel", "parallel"),
        name="hy_inproj",
    )(x, mod, g.reshape(1, d), w_in_t, b_in.reshape(n, 1))


def _short_conv(u, w):
    length = u.shape[1]
    lane = lax.broadcasted_iota(I32, (u.shape[0], LANES), 1)
    prev = pltpu.roll(u, 1, 1)
    nxt = pltpu.roll(u, length - 1, 1)
    prev = jnp.concatenate([jnp.where(lane == 0, 0.0, prev[:, :LANES]), prev[:, LANES:]], axis=1)
    nxt = jnp.concatenate([nxt[:, :length - LANES], jnp.where(lane == LANES - 1, 0.0, nxt[:, length - LANES:])], axis=1)
    return w[:, 0:1] * prev + w[:, 1:2] * u + w[:, 2:3] * nxt + w[:, 3:4]


def _hy_conv_gate_kernel(x0_ref, x1_ref, v_ref, w0_ref, w1_ref, wv_ref, z_ref, x0c_ref, *, group):
    ct, s = v_ref.shape[1], v_ref.shape[2]
    r = s // LANES
    z = _short_conv(v_ref[0], wv_ref[...]) * _short_conv(x1_ref[0], w1_ref[...])
    x0c = _short_conv(x0_ref[0], w0_ref[...])
    for g in range(ct // group):
        for j in range(r):
            rows = slice((g * r + j) * group, (g * r + j + 1) * group)
            chans = slice(g * group, (g + 1) * group)
            z_ref[0, rows, :] = z[chans, j * LANES:(j + 1) * LANES]
            x0c_ref[0, rows, :] = x0c[chans, j * LANES:(j + 1) * LANES]


def _hy_conv_gate(u_t, cwb, d, ct, group):
    b, _, s = u_t.shape
    nb = d // ct
    r = s // LANES
    u_spec = lambda p: pl.BlockSpec((1, ct, s), lambda i, j, p=p: (i, p * nb + j, 0))
    w_spec = lambda p: pl.BlockSpec((ct, 4), lambda i, j, p=p: (p * nb + j, 0))
    out_spec = pl.BlockSpec((1, ct * r, LANES), lambda i, j: (i, j, 0))
    return pl.pallas_call(
        functools.partial(_hy_conv_gate_kernel, group=group),
        grid=(b, nb),
        in_specs=[u_spec(0), u_spec(1), u_spec(2), w_spec(0), w_spec(1), w_spec(2)],
        out_specs=[out_spec, out_spec],
        out_shape=[jax.ShapeDtypeStruct((b, d * r, LANES), F32)] * 2,
        compiler_params=_params("parallel", "parallel"),
        name="hy_conv_gate",
    )(u_t, u_t, u_t, cwb, cwb, cwb)


def _hy_filter_kernel(w1t_ref, b1_ref, w2t_ref, b2_ref, w3t_ref, fr_ref, o_ref, *, length, lt, nfeat):
    half = pl.program_id(0)
    j = pl.program_id(1)
    d = o_ref.shape[0]
    rows = w1t_ref.shape[1]
    hi = lax.Precision.HIGHEST
    t = half * length + j * lt + lax.broadcasted_iota(I32, (rows, lt), 1)
    pos = jnp.where(half == 0, t, 2 * length - t).astype(F32)
    r = lax.broadcasted_iota(I32, (rows, lt), 0)
    tlin = pos / (length - 1.0)
    w = 2.0 * math.pi * pos / length
    fidx = jnp.where(r > HY_BANDS, r - (HY_BANDS + 1), r - 1).astype(F32)
    f = 1e-4 + fidx * ((HY_BANDS - 1 - 1e-4) / (HY_BANDS - 1))
    ang = f * w
    feat = jnp.where(r == 0, tlin, jnp.where(r <= HY_BANDS, jnp.cos(ang), -jnp.sin(ang)))
    feat = jnp.where(r < nfeat, feat, 0.0)
    fr = fr_ref[...]
    h = jnp.sin(fr * (jnp.dot(w1t_ref[...], feat, precision=hi, preferred_element_type=F32) + b1_ref[...]))
    h = jnp.sin(fr * (jnp.dot(w2t_ref[...], h, precision=hi, preferred_element_type=F32) + b2_ref[...]))
    out = jnp.dot(w3t_ref[...], h, precision=hi, preferred_element_type=F32)
    c = lax.broadcasted_iota(I32, (d, lt), 0).astype(F32)
    min_decay = math.log(HY_DECAY_TARGET) / HY_FAST_DECAY_PCT
    max_decay = math.log(HY_DECAY_TARGET) / HY_SLOW_DECAY_PCT
    delta = min_decay + c * ((max_decay - min_decay) / (d - 1.0))
    decay = jnp.exp(-tlin[0:1, :] * jnp.abs(delta))
    valid = pos[0:1, :] < length
    o_ref[...] = jnp.where(valid, out * decay, 0.0)


def _hy_filter(f_w1, f_b1, f_w2, f_b2, f_w3, f_freq, d, length):
    nfeat, hid = f_w1.shape
    rows = ((nfeat + 7) // 8) * 8
    w1t = jnp.zeros((hid, rows), F32).at[:, :nfeat].set(f_w1.T)
    lt = min(512, length)
    col = lambda a: a.reshape(hid, 1)
    small = lambda shape: pl.BlockSpec(shape, lambda h, j: (0, 0))
    return pl.pallas_call(
        functools.partial(_hy_filter_kernel, length=length, lt=lt, nfeat=nfeat),
        grid=(2, length // lt),
        in_specs=[small((hid, rows)), small((hid, 1)), small((hid, hid)), small((hid, 1)),
                  pl.BlockSpec((d, hid), lambda h, j: (h, 0)), small((hid, 1))],
        out_specs=pl.BlockSpec((d, lt), lambda h, j: (0, h * (length // lt) + j)),
        out_shape=jax.ShapeDtypeStruct((d, 2 * length), F32),
        compiler_params=_params("parallel", "parallel"),
        name="hy_filter",
    )(w1t, col(f_b1), f_w2.T, col(f_b2), f_w3.T, col(f_freq))


def _dft_tables(length):
    n = 2 * length
    n1 = n // LANES
    r = n1 // 2
    k1 = np.arange(n1)[:, None]
    ang1 = 2.0 * np.pi * k1 * np.arange(n1)[None, :] / n1
    c1, s1 = np.cos(ang1), np.sin(ang1)
    m1_real = np.concatenate([c1, -s1], axis=0)
    m1_cplx = np.block([[c1[:, :r], s1[:, :r]], [-s1[:, :r], c1[:, :r]]])
    m1_inv = np.block([[c1[:r, :], -s1[:r, :]], [s1[:r, :], c1[:r, :]]])
    ang_t = 2.0 * np.pi * k1 * np.arange(LANES)[None, :] / n
    tw = np.stack([np.cos(ang_t), np.sin(ang_t)])
    n2 = np.arange(LANES)
    ang2 = 2.0 * np.pi * n2[:, None] * n2[None, :] / LANES
    c2, s2 = np.cos(ang2), np.sin(ang2)
    w2_fwd = np.block([[c2, -s2], [s2, c2]])
    w2_inv = np.block([[c2, s2], [-s2, c2]])
    f = lambda a: jnp.asarray(a, F32)
    return dict(n1=n1, r=r, m1_real=f(m1_real), m1_cplx=f(m1_cplx), m1_inv=f(m1_inv),
                tw=f(tw), w2_fwd=f(w2_fwd), w2_inv=f(w2_inv))


def _wide(ref3, ct):
    return jnp.concatenate([ref3[c] for c in range(ct)], axis=1)


def _fft_fwd(x_wide, m1, tc, ts, w2, ct, n1, mm):
    a = mm(m1, x_wide)
    ar, ai = a[:n1], a[n1:]
    tall = []
    for c in range(ct):
        arc, aic = ar[:, c * LANES:(c + 1) * LANES], ai[:, c * LANES:(c + 1) * LANES]
        tall.append(jnp.concatenate([arc * tc + aic * ts, aic * tc - arc * ts], axis=1))
    return mm(jnp.concatenate(tall, axis=0), w2)


def _fft_inv(y_tall, w2i, tc, ts, m1i, ct, n1, mm):
    cc = mm(y_tall, w2i)
    dr, di = [], []
    for c in range(ct):
        cr, ci = cc[c * n1:(c + 1) * n1, :LANES], cc[c * n1:(c + 1) * n1, LANES:]
        dr.append(cr * tc - ci * ts)
        di.append(ci * tc + cr * ts)
    d_wide = jnp.concatenate([jnp.concatenate(dr, axis=1), jnp.concatenate(di, axis=1)], axis=0)
    return mm(m1i, d_wide)


def _hy_filter_fft_kernel(k_ref, m1_ref, tw_ref, w2_ref, kr_ref, ki_ref, *, ct, n1):
    x = _fft_fwd(_wide(k_ref, ct), m1_ref[...], tw_ref[0], tw_ref[1], w2_ref[...], ct, n1, _dot3)
    x = x * (1.0 / (n1 * LANES))
    kr_ref[...] = x[:, :LANES].reshape(ct, n1, LANES)
    ki_ref[...] = x[:, LANES:].reshape(ct, n1, LANES)


def _hy_filter_fft(k4, tabs, ct):
    d, n1, _ = k4.shape
    const = lambda a: pl.BlockSpec(a.shape, lambda i, nd=a.ndim: (0,) * nd)
    blk = pl.BlockSpec((ct, n1, LANES), lambda i: (i, 0, 0))
    return pl.pallas_call(
        functools.partial(_hy_filter_fft_kernel, ct=ct, n1=n1),
        grid=(d // ct,),
        in_specs=[blk, const(tabs["m1_real"]), const(tabs["tw"]), const(tabs["w2_fwd"])],
        out_specs=[blk, blk],
        out_shape=[jax.ShapeDtypeStruct((d, n1, LANES), F32)] * 2,
        compiler_params=_params("parallel"),
        name="hy_filter_fft",
    )(k4, tabs["m1_real"], tabs["tw"], tabs["w2_fwd"])


def _hy_longconv_kernel(z_ref, x0_ref, kr_ref, ki_ref, bias_ref, m1_ref, m1i_ref, tw_ref, w2_ref, w2i_ref,
                        o_ref, y_ref, *, ct, n1, r):
    tc, ts = tw_ref[0], tw_ref[1]
    def chan(ref, p, c):
        return ref[p, pl.ds(c, r, stride=ct), :]

    x_wide = jnp.concatenate([jnp.concatenate([chan(z_ref, p, c) for c in range(ct)], axis=1) for p in range(2)],
                             axis=0)
    x = _fft_fwd(x_wide, m1_ref[...], tc, ts, w2_ref[...], ct, n1, _dot1)
    xr, xi = x[:, :LANES], x[:, LANES:]
    kr = kr_ref[...].reshape(ct * n1, LANES)
    ki = ki_ref[...].reshape(ct * n1, LANES)
    y_tall = jnp.concatenate([xr * kr - xi * ki, xr * ki + xi * kr], axis=1)
    y = _fft_inv(y_tall, w2i_ref[...], tc, ts, m1i_ref[...], ct, n1, _dot1)
    for p in range(2):
        for c in range(ct):
            conv = y[p * r:(p + 1) * r, c * LANES:(c + 1) * LANES]
            y_ref[p, c * r:(c + 1) * r, :] = (conv + bias_ref[c] * chan(z_ref, p, c)) * chan(x0_ref, p, c)
        for j in range(r):
            o_ref[p, :, j * LANES:(j + 1) * LANES] = y_ref[p, pl.ds(j, ct, stride=r), :]


def _hy_longconv(z, x0c, kr, ki, bias, tabs, ct):
    b, d = z.shape[0], bias.shape[0]
    n1, r = tabs["n1"], tabs["r"]
    const = lambda a: pl.BlockSpec(a.shape, lambda i, j, nd=a.ndim: (0,) * nd)
    data = pl.BlockSpec((2, ct * r, LANES), lambda i, j: (j, i, 0))
    spec = pl.BlockSpec((ct, n1, LANES), lambda i, j: (i, 0, 0))
    consts = [tabs["m1_cplx"], tabs["m1_inv"], tabs["tw"], tabs["w2_fwd"], tabs["w2_inv"]]
    return pl.pallas_call(
        functools.partial(_hy_longconv_kernel, ct=ct, n1=n1, r=r),
        grid=(d // ct, b // 2),
        in_specs=[data, data, spec, spec, pl.BlockSpec((ct, 1, 1), lambda i, j: (i, 0, 0))]
                 + [const(a) for a in consts],
        out_specs=pl.BlockSpec((2, ct, r * LANES), lambda i, j: (j, i, 0)),
        out_shape=jax.ShapeDtypeStruct((b, d, r * LANES), F32),
        scratch_shapes=[pltpu.VMEM((2, ct * r, LANES), F32)],
        compiler_params=_params("parallel", "parallel"),
        name="hy_longconv",
    )(z, x0c, kr, ki, bias.reshape(d, 1, 1), *consts)


def _router_tail(x1, m, gf_ref, wrt_ref, x1_ref, hf_ref, aff_ref):
    x1_ref[0] = x1
    hf = _norm_mod(x1, gf_ref[...], m[3:4], m[4:5]).astype(BF16)
    hf_ref[0] = hf
    logits = _dot_nt(wrt_ref[...], hf)
    ex = jnp.exp(logits - jnp.max(logits, axis=0, keepdims=True))
    aff_ref[0] = ex / jnp.sum(ex, axis=0, keepdims=True)


def _hy_outproj_kernel(y_ref, x_ref, m_ref, w_ref, b_ref, gf_ref, wrt_ref, x1_ref, hf_ref, aff_ref):
    m = m_ref[0]
    y = jnp.transpose(y_ref[0]).astype(BF16)
    x1 = x_ref[0] + m[2:3] * (_dot(y, w_ref[...]) + b_ref[...])
    _router_tail(x1, m, gf_ref, wrt_ref, x1_ref, hf_ref, aff_ref)


def _mla_outproj_kernel(y_ref, x_ref, m_ref, w_ref, gf_ref, wrt_ref, x1_ref, hf_ref, aff_ref):
    m = m_ref[0]
    x1 = x_ref[0] + m[2:3] * _dot(y_ref[0].astype(BF16), w_ref[...])
    _router_tail(x1, m, gf_ref, wrt_ref, x1_ref, hf_ref, aff_ref)


def _outproj(y, x, mod, w_out, b_out, g_ffn, w_router_t, tt, channel_major):
    b, s, d = x.shape
    e = w_router_t.shape[0]
    dy = w_out.shape[0]
    tile = pl.BlockSpec((1, tt, d), lambda i, j: (i, j, 0))
    const = lambda shape: pl.BlockSpec(shape, lambda i, j: (0, 0))
    if channel_major:
        y_spec = pl.BlockSpec((1, dy, tt), lambda i, j: (i, 0, j))
        kern, extra, extra_specs = _hy_outproj_kernel, [b_out.reshape(1, d)], [const((1, d))]
    else:
        y_spec = pl.BlockSpec((1, tt, dy), lambda i, j: (i, j, 0))
        kern, extra, extra_specs = _mla_outproj_kernel, [], []
    return pl.pallas_call(
        kern,
        grid=(b, s // tt),
        in_specs=[y_spec, tile, pl.BlockSpec((1, 6, d), lambda i, j: (i, 0, 0)), const((dy, d))]
                 + extra_specs + [const((1, d)), const((e, d))],
        out_specs=[tile, tile, pl.BlockSpec((1, e, tt), lambda i, j: (i, 0, j))],
        out_shape=[jax.ShapeDtypeStruct((b, s, d), F32), jax.ShapeDtypeStruct((b, s, d), BF16),
                   jax.ShapeDtypeStruct((b, e, s), F32)],
        compiler_params=_params("parallel", "parallel"),
        name="outproj_router",
    )(y, x, mod, w_out, *extra, g_ffn.reshape(1, d), w_router_t)


def _route_kernel(a_ref, o_ref, idx_ref, cs_ref, *, n_exp, n_chunk, cap):
    rows = n_exp * n_chunk
    a = a_ref[0].reshape(rows, LANES)
    ri = lax.broadcasted_iota(I32, (rows, rows), 0)
    rj = lax.broadcasted_iota(I32, (rows, rows), 1)
    same = (ri // n_chunk) == (rj // n_chunk)
    blk_all = jnp.where(same, 1.0, 0.0).astype(BF16)
    blk_before = jnp.where(same & (rj < ri), 1.0, 0.0).astype(BF16)
    li = lax.broadcasted_iota(I32, (LANES, LANES), 0)
    lj = lax.broadcasted_iota(I32, (LANES, LANES), 1)
    ones = jnp.ones((LANES, LANES), BF16)
    before = jnp.where(li < lj, 1.0, 0.0).astype(BF16)

    def as_bf16(mask):
        return jnp.where(mask, 1.0, 0.0).astype(BF16)

    def expert_count(mask):
        return _dot(blk_all, _dot(as_bf16(mask), ones).astype(BF16))

    def prefix(mask):
        mb = as_bf16(mask)
        return _dot(mb, before) + _dot(blk_before, _dot(mb, ones).astype(BF16))

    def enough(threshold):
        return expert_count(a >= threshold) >= cap

    cur = jnp.zeros((rows, LANES), I32)
    for bit in range(30, 0, -2):
        b1, b0 = 1 << bit, 1 << (bit - 1)
        c1, c0, c10 = cur | b1, cur | b0, cur | (b1 | b0)
        t1, t0, t10 = (enough(pltpu.bitcast(c, F32)) for c in (c1, c0, c10))
        cur = jnp.where(t10, c10, jnp.where(t1, c1, jnp.where(t0, c0, cur)))
    cur = jnp.where(enough(pltpu.bitcast(cur | 1, F32)), cur | 1, cur)
    lo = pltpu.bitcast(cur, F32)
    hi = pltpu.bitcast(cur + 1, F32)
    for _ in range(2):
        width = hi - lo
        m1, m2, m3 = lo + width * 0.25, lo + width * 0.5, lo + width * 0.75
        t1, t2, t3 = enough(m1), enough(m2), enough(m3)
        lo, hi = (jnp.where(t3, m3, jnp.where(t2, m2, jnp.where(t1, m1, lo))),
                  jnp.where(t1, jnp.where(t2, jnp.where(t3, hi, m3), m2), m1))
    gt = a >= hi
    eq = (a >= lo) & (a < hi)
    need = cap - expert_count(gt)
    sel = gt | (eq & (prefix(eq) < need))
    sel_b = as_bf16(sel)
    chunk_total = _dot(sel_b, ones)
    chunk_start = _dot(blk_before, chunk_total.astype(BF16))
    within = _dot(sel_b, before)
    o_ref[0] = jnp.where(sel, (chunk_start + within).astype(I32), -1).reshape(n_exp, n_chunk, LANES)
    cs_ref[0] = chunk_start[:, 0:1].astype(I32).reshape(n_exp, n_chunk, 1)

    rank = jnp.where(sel, within + 1.0, 0.0).astype(BF16)
    start_row = jnp.transpose(chunk_start)[0:1, :]
    end_row = jnp.transpose(chunk_start + chunk_total)[0:1, :]
    s_col = lax.broadcasted_iota(I32, (cap, rows), 0).astype(F32)
    row_id = lax.broadcasted_iota(I32, (cap, rows), 1)
    lane_f = lax.broadcasted_iota(I32, (cap, LANES), 1).astype(F32)
    for ex in range(n_exp):
        in_chunk = ((row_id // n_chunk) == ex) & (start_row <= s_col) & (s_col < end_row)
        rank_rows = _dot(jnp.where(in_chunk, 1.0, 0.0).astype(BF16), rank)
        start_s = jnp.sum(jnp.where(in_chunk, start_row, 0.0), axis=1, keepdims=True)
        chunk_s = jnp.sum(jnp.where(in_chunk, (row_id - ex * n_chunk).astype(F32), 0.0), axis=1, keepdims=True)
        match = rank_rows == (s_col[:, 0:1] + 1.0 - start_s)
        lane_s = jnp.sum(jnp.where(match, lane_f, 0.0), axis=1, keepdims=True)
        idx_ref[0, ex] = (chunk_s * LANES + lane_s).astype(I32)


def _route(aff_t, cap):
    b, e, t = aff_t.shape
    nc = t // LANES
    blk = pl.BlockSpec((1, e, nc, LANES), lambda i: (i, 0, 0, 0))
    slot, idx, chunk_start = pl.pallas_call(
        functools.partial(_route_kernel, n_exp=e, n_chunk=nc, cap=cap),
        grid=(b,),
        in_specs=[blk],
        out_specs=[blk, pl.BlockSpec((1, e, cap, 1), lambda i: (i, 0, 0, 0)),
                   pl.BlockSpec((1, e, nc, 1), lambda i: (i, 0, 0, 0))],
        out_shape=[jax.ShapeDtypeStruct((b, e, nc, LANES), I32), jax.ShapeDtypeStruct((b, e, cap, 1), I32),
                   jax.ShapeDtypeStruct((b, e, nc, 1), I32)],
        compiler_params=_params("parallel"),
        name="route_topc",
    )(aff_t.reshape(b, e, nc, LANES))
    return slot.reshape(b, e, t), idx.reshape(b, e, cap), chunk_start.reshape(b, e, nc)


SCATTER_UNROLL = 8


def _expert_kernel(idx_ref, cs_ref, hf_ref, slot_ref, aff_ref, wg_ref, wu_ref, wd_ref, acc_ref, ye_ref, xs_ref,
                   gsl_ref, *, cap, tk, win):
    t, d = hf_ref.shape[1], hf_ref.shape[2]
    nc = d // LANES

    @pl.when(pl.program_id(1) == 0)
    def _():
        acc_ref[...] = jnp.zeros_like(acc_ref)

    xs_ref[...] = jnp.zeros_like(xs_ref)
    gsl_ref[...] = jnp.zeros_like(gsl_ref)
    n_tk = t // tk
    for k in range(n_tk):
        cols = slice(k * tk, (k + 1) * tk)

        def gather(n_rows, first):
            hit = (lax.broadcasted_iota(I32, (n_rows, tk), 0) + first) == slot_ref[0, 0, :, cols]
            x = _dot(jnp.where(hit, 1.0, 0.0).astype(BF16), hf_ref[0, cols, :])
            g = jnp.sum(jnp.where(hit, aff_ref[0, 0, :, cols], 0.0), axis=1, keepdims=True)
            return x, g

        lo = cs_ref[0, 0, 0, k * (tk // LANES)]
        hi = cs_ref[0, 0, 0, (k + 1) * (tk // LANES)] if k + 1 < n_tk else cap
        first = jnp.minimum((lo // 8) * 8, cap - win)
        fits = (hi - first) <= win

        @pl.when(fits)
        def _():
            x, g = gather(win, first)
            rows = pl.ds(pl.multiple_of(first, 8), win)
            xs_ref[rows, :] += x
            gsl_ref[rows, :] += g

        @pl.when(jnp.logical_not(fits))
        def _():
            x, g = gather(cap, 0)
            xs_ref[...] += x
            gsl_ref[...] += g

    xs = xs_ref[...]
    gsl = gsl_ref[...]
    xs = xs.astype(BF16)
    gate = _dot(xs, wg_ref[0])
    up = _dot(xs, wu_ref[0])
    hid = gate / (1.0 + jnp.exp(-gate)) * up
    ye = _dot(hid.astype(BF16), wd_ref[0]) * gsl
    for c in range(nc):
        ye_ref[pl.ds(c, cap, stride=nc), :] = ye[:, c * LANES:(c + 1) * LANES]

    def tile_of(i):
        return pl.ds(pl.multiple_of(i * nc, nc), nc)

    def scatter_group(g, carry):
        base = g * SCATTER_UNROLL
        toks = [idx_ref[0, 0, 0, base + u] for u in range(SCATTER_UNROLL)]
        sums = [acc_ref[0, tile_of(toks[u]), :] + ye_ref[tile_of(base + u), :] for u in range(SCATTER_UNROLL)]
        for u in range(SCATTER_UNROLL):
            acc_ref[0, tile_of(toks[u]), :] = sums[u]
        return carry

    lax.fori_loop(0, cap // SCATTER_UNROLL, scatter_group, 0)


def _experts(hf, idx, chunk_start, slot_row, aff_row, wg, wu, wd, cap):
    b, t, d = hf.shape
    e, _, f = wg.shape
    nc = d // LANES
    tk = max(LANES, t // 8)
    win = max(8, cap // 4)
    row = pl.BlockSpec((1, 1, 1, t), lambda i, j: (i, j, 0, 0))
    smem_row = lambda n: pl.BlockSpec((1, 1, 1, n), lambda i, j: (i, j, 0, 0), memory_space=pltpu.SMEM)
    once = pl.Buffered(1)
    return pl.pallas_call(
        functools.partial(_expert_kernel, cap=cap, tk=tk, win=win),
        grid=(b, e),
        in_specs=[smem_row(cap), smem_row(t // LANES),
                  pl.BlockSpec((1, t, d), lambda i, j: (i, 0, 0), pipeline_mode=once), row, row,
                  pl.BlockSpec((1, d, f), lambda i, j: (j, 0, 0)),
                  pl.BlockSpec((1, d, f), lambda i, j: (j, 0, 0)),
                  pl.BlockSpec((1, f, d), lambda i, j: (j, 0, 0))],
        out_specs=pl.BlockSpec((1, t * nc, LANES), lambda i, j: (i, 0, 0), pipeline_mode=once),
        out_shape=jax.ShapeDtypeStruct((b, t * nc, LANES), F32),
        scratch_shapes=[pltpu.VMEM((cap * nc, LANES), F32), pltpu.VMEM((cap, d), F32), pltpu.VMEM((cap, 1), F32)],
        compiler_params=_params("parallel", "arbitrary"),
        name="expert_ffn",
    )(idx.reshape(b, e, 1, cap), chunk_start.reshape(b, e, 1, t // LANES), hf, slot_row, aff_row, wg, wu, wd)


def _combine_kernel(acc_ref, x_ref, m_ref, gn_ref, o_ref, *, final_norm):
    tt, d = o_ref.shape[1], o_ref.shape[2]
    nc = d // LANES
    moe = jnp.concatenate([acc_ref[0, pl.ds(c, tt, stride=nc), :] for c in range(nc)], axis=1)
    x2 = x_ref[0] + m_ref[0][5:6] * moe
    o_ref[0] = _rms(x2, gn_ref[...]) if final_norm else x2


def _combine(acc, x1, mod, g_final, tt, final_norm):
    b, t, d = x1.shape
    nc = d // LANES
    tile = pl.BlockSpec((1, tt, d), lambda i, j: (i, j, 0))
    return pl.pallas_call(
        functools.partial(_combine_kernel, final_norm=final_norm),
        grid=(b, t // tt),
        in_specs=[pl.BlockSpec((1, tt * nc, LANES), lambda i, j: (i, j, 0)),
                  tile, pl.BlockSpec((1, 6, d), lambda i, j: (i, 0, 0)),
                  pl.BlockSpec((1, d), lambda i, j: (0, 0))],
        out_specs=tile,
        out_shape=jax.ShapeDtypeStruct((b, t, d), F32),
        compiler_params=_params("parallel", "parallel"),
        name="moe_combine",
    )(acc, x1, mod, g_final.reshape(1, d))


def _moe(hf, aff_t, x1, mod, wg, wu, wd, g_final, final_norm):
    b, t, d = hf.shape
    e = aff_t.shape[1]
    cap = EC_CAPACITY_FACTOR * t // e
    slot, idx, chunk_start = _route(aff_t, cap)
    acc = _experts(hf, idx, chunk_start, slot.reshape(b, e, 1, t), aff_t.reshape(b, e, 1, t), wg, wu, wd, cap)
    return _combine(acc, x1, mod, g_final, min(512, t), final_norm)


def _rope(x, cos_t, sin_lo, sin_hi):
    return x * cos_t + pltpu.roll(x, LANES - QK_ROPE // 2, 1) * sin_lo + pltpu.roll(x, QK_ROPE // 2, 1) * sin_hi


def _mla_proj_kernel(x_ref, m_ref, g_ref, pos_ref, invf_ref, win_ref, qg_ref, wq_ref, kg_ref, wkn_ref, wv_ref,
                     q_ref, k_ref, v_ref):
    m = m_ref[0]
    h = _norm_mod(x_ref[0], g_ref[...], m[0:1], m[1:2]).astype(BF16)
    a = _dot(h, win_ref[...])
    cq = a[:, :Q_LORA]
    ckv = a[:, Q_LORA:Q_LORA + KV_LORA]
    kpe = a[:, Q_LORA + KV_LORA:]
    q = _dot(_rms(cq, qg_ref[...]).astype(BF16), wq_ref[...])
    kvn = _rms(ckv, kg_ref[...]).astype(BF16)
    kn = _dot(kvn, wkn_ref[...])
    v_ref[0] = _dot(kvn, wv_ref[...]).astype(BF16)
    ang = pos_ref[0] * invf_ref[...]
    lane = lax.broadcasted_iota(I32, ang.shape, 1)
    cos_t = jnp.where(lane < QK_ROPE, jnp.cos(ang), 0.0)
    sin_a = jnp.sin(ang)
    sin_lo = jnp.where(lane < QK_ROPE // 2, -sin_a, 0.0)
    sin_hi = jnp.where((lane >= QK_ROPE // 2) & (lane < QK_ROPE), sin_a, 0.0)
    kpe = _rope(kpe, cos_t, sin_lo, sin_hi)
    qs, ks = [], []
    for hd in range(MLA_HEADS):
        qs.append(q[:, hd * HEAD_PAD:hd * HEAD_PAD + QK_NOPE])
        qs.append(_rope(q[:, hd * HEAD_PAD + QK_NOPE:(hd + 1) * HEAD_PAD], cos_t, sin_lo, sin_hi))
        ks.append(kn[:, hd * QK_NOPE:(hd + 1) * QK_NOPE])
        ks.append(kpe)
    q_ref[0] = jnp.concatenate(qs, axis=1).astype(BF16)
    k_ref[0] = jnp.concatenate(ks, axis=1).astype(BF16)


def _mla_proj(x, mod, g, positions, w_in, q_g, w_qb, kv_g, w_kvb, tt):
    b, s, d = x.shape
    nh = MLA_HEADS
    pad = HEAD_PAD - QK_NOPE - QK_ROPE
    win = jnp.concatenate([w_in, jnp.zeros((d, LANES - QK_ROPE), F32)], axis=1).astype(BF16)
    wq = w_qb.reshape(Q_LORA, nh, QK_NOPE + QK_ROPE)
    wq = jnp.concatenate([wq, jnp.zeros((Q_LORA, nh, pad), F32)], axis=2).reshape(Q_LORA, nh * HEAD_PAD).astype(BF16)
    wkv = w_kvb.reshape(KV_LORA, nh, QK_NOPE + V_HEAD)
    wkn = wkv[:, :, :QK_NOPE].reshape(KV_LORA, nh * QK_NOPE).astype(BF16)
    wv = wkv[:, :, QK_NOPE:].reshape(KV_LORA, nh * V_HEAD).astype(BF16)
    inv_freq = ROPE_THETA ** (-jnp.arange(0, QK_ROPE, 2, dtype=F32) / QK_ROPE)
    invf = jnp.concatenate([inv_freq, inv_freq, jnp.zeros((LANES - QK_ROPE,), F32)]).reshape(1, LANES)
    pos = positions.astype(F32).reshape(b, s, 1)
    const = lambda a: pl.BlockSpec(a.shape, lambda i, j: (0, 0))
    tile = lambda w: pl.BlockSpec((1, tt, w), lambda i, j: (i, j, 0))
    g2, qg2, kg2 = g.reshape(1, d), q_g.reshape(1, Q_LORA), kv_g.reshape(1, KV_LORA)
    return pl.pallas_call(
        _mla_proj_kernel,
        grid=(b, s // tt),
        in_specs=[tile(d), pl.BlockSpec((1, 6, d), lambda i, j: (i, 0, 0)), const(g2), tile(1), const(invf),
                  const(win), const(qg2), const(wq), const(kg2), const(wkn), const(wv)],
        out_specs=[tile(nh * HEAD_PAD), tile(nh * HEAD_PAD), tile(nh * V_HEAD)],
        out_shape=[jax.ShapeDtypeStruct((b, s, nh * HEAD_PAD), BF16),
                   jax.ShapeDtypeStruct((b, s, nh * HEAD_PAD), BF16),
                   jax.ShapeDtypeStruct((b, s, nh * V_HEAD), BF16)],
        compiler_params=_params("parallel", "parallel"),
        name="mla_proj",
    )(x, mod, g2, pos, invf, win, qg2, wq, kg2, wkn, wv)


def _mla_attn_kernel(q_ref, k_ref, v_ref, o_ref, *, n_sub):
    sub = q_ref.shape[1] // n_sub
    for i in range(n_sub):
        s = _dot_nt(q_ref[0, i * sub:(i + 1) * sub, :], k_ref[0])
        p = jnp.exp2((s - jnp.max(s, axis=1, keepdims=True)) * (SOFTMAX_SCALE * math.log2(math.e)))
        o = _dot(p.astype(BF16), v_ref[0])
        o_ref[0, i * sub:(i + 1) * sub, :] = o / jnp.sum(p, axis=1, keepdims=True)


def _mla_attn(q, k, v, tq):
    b, s, _ = q.shape
    nh = MLA_HEADS
    return pl.pallas_call(
        functools.partial(_mla_attn_kernel, n_sub=max(1, tq // 256)),
        grid=(b, nh, s // tq),
        in_specs=[pl.BlockSpec((1, tq, HEAD_PAD), lambda i, h, j: (i, j, h)),
                  pl.BlockSpec((1, s, HEAD_PAD), lambda i, h, j: (i, 0, h)),
                  pl.BlockSpec((1, s, V_HEAD), lambda i, h, j: (i, 0, h))],
        out_specs=pl.BlockSpec((1, tq, V_HEAD), lambda i, h, j: (i, j, h)),
        out_shape=jax.ShapeDtypeStruct((b, s, nh * V_HEAD), F32),
        compiler_params=_params("parallel", "parallel", "parallel"),
        name="mla_attn",
    )(q, k, v)


def kernel(x, c, positions, ada_w, ada_b, norm_mix_g, norm_ffn_g, hy_w_in, hy_b_in, hy_conv_w, hy_conv_b, hy_f_w1, hy_f_b1, hy_f_w2, hy_f_b2, hy_f_w3, hy_f_freq, hy_f_bias, hy_w_out, hy_b_out, mla_w_in, mla_q_norm_g, mla_w_qb, mla_kv_norm_g, mla_w_kvb, mla_w_out, moe_w_router, moe_w_gate, moe_w_up, moe_w_down, final_norm_g):
    b, s, d = x.shape
    depth = ada_w.shape[0]
    assert s % LANES == 0 and b % 2 == 0 and depth == 2
    tt = min(512, s)
    mod = _ada(c, ada_w, ada_b).reshape(depth, b, 6, d)

    u_t = _hy_inproj(x, mod[0], norm_mix_g[0], hy_w_in[0].T.astype(BF16), hy_b_in[0], tt)
    cwb = jnp.concatenate([hy_conv_w[0].T, hy_conv_b[0][:, None]], axis=1)
    ct = min(16, d)
    z_g, x0c_g = _hy_conv_gate(u_t, cwb, d, min(64, d), ct)
    tabs = _dft_tables(s)
    n1, r = tabs["n1"], tabs["r"]
    k_t = _hy_filter(hy_f_w1[0], hy_f_b1[0], hy_f_w2[0], hy_f_b2[0], hy_f_w3[0], hy_f_freq[0], d, s)
    ct = min(16, d)
    kr, ki = _hy_filter_fft(k_t.reshape(d, n1, LANES), tabs, ct)
    yg = _hy_longconv(z_g, x0c_g, kr, ki, hy_f_bias[0, 0], tabs, ct)
    x1, hf, aff_t = _outproj(yg, x, mod[0], hy_w_out[0].astype(BF16), hy_b_out[0], norm_ffn_g[0],
                             moe_w_router[0].T.astype(BF16), tt, True)
    x2 = _moe(hf, aff_t, x1, mod[0], moe_w_gate[0].astype(BF16), moe_w_up[0].astype(BF16),
              moe_w_down[0].astype(BF16), final_norm_g, False)

    q, k, v = _mla_proj(x2, mod[1], norm_mix_g[1], positions, mla_w_in[0], mla_q_norm_g[0], mla_w_qb[0],
                        mla_kv_norm_g[0], mla_w_kvb[0], tt)
    o = _mla_attn(q, k, v, min(2048, s))
    x3, hf, aff_t = _outproj(o, x2, mod[1], mla_w_out[0].astype(BF16), None, norm_ffn_g[1],
                             moe_w_router[1].T.astype(BF16), tt, False)
    return _moe(hf, aff_t, x3, mod[1], moe_w_gate[1].astype(BF16), moe_w_up[1].astype(BF16),
                moe_w_down[1].astype(BF16), final_norm_g, True)
```
